```python
import math
import jax, jax.numpy as jnp
from jax import lax
import numpy as np

D_MODEL = 1024
BATCH = 8
SEQ = 4096
DEPTH = 2

CHUNK = 64
MEM_LEN = 256
N_MIXERS = 2
N_A = (DEPTH + 1) // 2
N_B = DEPTH // 2

D_LRU = D_MODEL // 2
LRU_BLOCKS = 8
LRU_BLK = D_LRU // LRU_BLOCKS
CONV_WIDTH = 4
LRU_C = 8.0

FOX_HEADS = 8
FOX_HD = 64
FOX_WIDTH = FOX_HEADS * FOX_HD
Q_BLOCK = 128

XA_HEADS = 4
XA_HD = 128
XA_WIDTH = XA_HEADS * XA_HD

D_FF = 4 * D_MODEL
EPS = 1e-6
NEG_INF = -1e30

kernel_name = "hybrid_rglru_fox_memxattn_trunk"


def rms_norm(x, g):
    xf = x.astype(jnp.float32)
    y = xf * lax.rsqrt(jnp.mean(xf * xf, axis=-1, keepdims=True) + EPS)
    return (y * g.astype(jnp.float32)).astype(x.dtype)


def _lin_combine(left, right):
    a1, b1 = left
    a2, b2 = right
    return a1 * a2, a2 * b1 + b2


def rglru_group(h, w_in, conv_w, conv_b, w_r, b_r, w_i, b_i, lam):
    B, S, _ = h.shape
    proj = h @ w_in
    u, gate, xq = jnp.split(proj, [D_LRU, 2 * D_LRU], axis=-1)
    u = lax.conv_general_dilated(
        u, conv_w[:, None, :].astype(u.dtype), window_strides=(1,),
        padding=[(CONV_WIDTH - 1, 0)], dimension_numbers=("NWC", "WIO", "NWC"),
        feature_group_count=D_LRU) + conv_b
    ub = u.reshape(B, S, LRU_BLOCKS, LRU_BLK)
    r = jax.nn.sigmoid(jnp.einsum("bsnc,ncd->bsnd", ub, w_r).reshape(B, S, D_LRU) + b_r)
    i = jax.nn.sigmoid(jnp.einsum("bsnc,ncd->bsnd", ub, w_i).reshape(B, S, D_LRU) + b_i)
    log_a = -LRU_C * r.astype(jnp.float32) * jax.nn.softplus(-lam.astype(jnp.float32))
    a = jnp.exp(log_a)
    b = jnp.sqrt(-jnp.expm1(2.0 * log_a)) * (i.astype(jnp.float32) * u.astype(jnp.float32))
    _, hs = lax.associative_scan(_lin_combine, (a, b), axis=1)
    y = hs.astype(h.dtype) * jax.nn.gelu(gate, approximate=True)
    return y, xq


def fox_group(h, w_in, b_f, q_gain, k_gain):
    B, S, _ = h.shape
    n_qb = S // Q_BLOCK
    proj = h @ w_in
    q, k, v, f_logit, xq = jnp.split(
        proj, [FOX_WIDTH, 2 * FOX_WIDTH, 3 * FOX_WIDTH, 3 * FOX_WIDTH + FOX_HEADS], axis=-1)
    q = rms_norm(q.reshape(B, S, FOX_HEADS, FOX_HD), q_gain)
    k = rms_norm(k.reshape(B, S, FOX_HEADS, FOX_HD), k_gain)
    v = v.reshape(B, S, FOX_HEADS, FOX_HD)
    log_f = jax.nn.log_sigmoid(f_logit.astype(jnp.float32) + b_f.astype(jnp.float32))
    c = jnp.cumsum(log_f, axis=1).transpose(0, 2, 1)
    kh = k.transpose(0, 2, 1, 3)
    vh = v.transpose(0, 2, 1, 3)
    q_blocks = q.reshape(B, n_qb, Q_BLOCK, FOX_HEADS, FOX_HD).transpose(1, 0, 3, 2, 4)
    c_blocks = c.reshape(B, FOX_HEADS, n_qb, Q_BLOCK).transpose(2, 0, 1, 3)
    q_pos = jnp.arange(S, dtype=jnp.int32).reshape(n_qb, Q_BLOCK)
    k_pos = jnp.arange(S, dtype=jnp.int32)
    scale = 1.0 / math.sqrt(FOX_HD)

    def one_block(args):
        qb, cb, pos = args
        s = jnp.einsum("bhqd,bhkd->bhqk", qb, kh).astype(jnp.float32) * scale
        s = s + cb[..., :, None] - c[:, :, None, :]
        s = jnp.where(pos[:, None] >= k_pos[None, :], s, NEG_INF)
        p = jax.nn.softmax(s, axis=-1)
        return jnp.einsum("bhqk,bhkd->bhqd", p.astype(vh.dtype), vh)

    o = lax.map(one_block, (q_blocks, c_blocks, q_pos))
    o = o.transpose(1, 0, 3, 2, 4).reshape(B, S, FOX_WIDTH)
    return o, xq


def memory_group(xq, mem, g_mem, w_kv, q_gain, k_gain):
    B, S, _ = xq.shape
    M = mem.shape[1]
    kv = rms_norm(mem, g_mem) @ w_kv
    k, v = jnp.split(kv, [XA_WIDTH], axis=-1)
    q = rms_norm(xq.reshape(B, S, XA_HEADS, XA_HD), q_gain)
    k = rms_norm(k.reshape(B, M, XA_HEADS, XA_HD), k_gain)
    v = v.reshape(B, M, XA_HEADS, XA_HD)
    s = jnp.einsum("bshd,bmhd->bhsm", q, k).astype(jnp.float32) * (1.0 / math.sqrt(XA_HD))
    p = jax.nn.softmax(s, axis=-1)
    o = jnp.einsum("bhsm,bmhd->bshd", p.astype(v.dtype), v)
    return o.reshape(B, S, XA_WIDTH)


def sqrelu_mlp(h, w1, w2):
    return jnp.square(jax.nn.relu(h @ w1)) @ w2


def setup_inputs(seed: int = 0) -> dict:
    key = jax.random.key(seed)
    ks = jax.random.split(key, 32)
    f32 = jnp.float32

    def nrm(k, shape, fan_in):
        return jax.random.normal(k, shape, f32) * (fan_in ** -0.5)

    def gain(k, shape):
        return 1.0 + 0.05 * jax.random.normal(k, shape, f32)

    def small(k, shape):
        return 0.02 * jax.random.normal(k, shape, f32)

    x = jax.random.normal(ks[0], (BATCH, SEQ, D_MODEL), f32)
    mem = jax.random.normal(ks[1], (BATCH, MEM_LEN, D_MODEL), f32)
    a_c = jax.random.uniform(ks[2], (N_A, D_LRU), f32, 0.9, 0.999)
    s_lam = a_c ** (1.0 / LRU_C)
    lru_lambda = jnp.log(s_lam) - jnp.log1p(-s_lam)
    return {
        "x": x,
        "mem": mem,
        "norm_mix": gain(ks[3], (DEPTH, D_MODEL)),
        "norm_mem": gain(ks[4], (DEPTH, D_MODEL)),
        "w_mem_kv": nrm(ks[5], (DEPTH, D_MODEL, 2 * XA_WIDTH), D_MODEL),
        "xa_q_gain": gain(ks[6], (DEPTH, XA_HD)),
        "xa_k_gain": gain(ks[7], (DEPTH, XA_HD)),
        "w_out": nrm(ks[8], (DEPTH, D_LRU + XA_WIDTH, D_MODEL), D_LRU + XA_WIDTH),
        "norm_mlp": gain(ks[9], (DEPTH, D_MODEL)),
        "w_mlp_in": nrm(ks[10], (DEPTH, D_MODEL, D_FF), D_MODEL),
        "w_mlp_out": nrm(ks[11], (DEPTH, D_FF, D_MODEL), D_FF),
        "w_in_a": nrm(ks[12], (N_A, D_MODEL, 2 * D_LRU + XA_WIDTH), D_MODEL),
        "conv_w": nrm(ks[13], (N_A, CONV_WIDTH, D_LRU), CONV_WIDTH),
        "conv_b": small(ks[14], (N_A, D_LRU)),
        "w_rgate": nrm(ks[15], (N_A, LRU_BLOCKS, LRU_BLK, LRU_BLK), LRU_BLK),
        "b_rgate": small(ks[16], (N_A, D_LRU)),
        "w_igate": nrm(ks[17], (N_A, LRU_BLOCKS, LRU_BLK, LRU_BLK), LRU_BLK),
        "b_igate": small(ks[18], (N_A, D_LRU)),
        "lru_lambda": lru_lambda,
        "w_in_b": nrm(ks[19], (N_B, D_MODEL, 3 * FOX_WIDTH + FOX_HEADS + XA_WIDTH), D_MODEL),
        "b_forget": jax.random.uniform(ks[20], (N_B, FOX_HEADS), f32, 1.0, 6.0),
        "fox_q_gain": gain(ks[21], (N_B, FOX_HD)),
        "fox_k_gain": gain(ks[22], (N_B, FOX_HD)),
    }


def reference(x, mem, norm_mix, norm_mem, w_mem_kv, xa_q_gain, xa_k_gain, w_out,
              norm_mlp, w_mlp_in, w_mlp_out, w_in_a, conv_w, conv_b, w_rgate, b_rgate,
              w_igate, b_igate, lru_lambda, w_in_b, b_forget, fox_q_gain, fox_k_gain):
    for layer in range(DEPTH):
        h = rms_norm(x, norm_mix[layer])
        j = layer // N_MIXERS
        if layer % N_MIXERS == 0:
            y_mix, xq = rglru_group(h, w_in_a[j], conv_w[j], conv_b[j], w_rgate[j], b_rgate[j],
                                    w_igate[j], b_igate[j], lru_lambda[j])
        else:
            y_mix, xq = fox_group(h, w_in_b[j], b_forget[j], fox_q_gain[j], fox_k_gain[j])
        y_mem = memory_group(xq, mem, norm_mem[layer], w_mem_kv[layer],
                             xa_q_gain[layer], xa_k_gain[layer])
        x = x + jnp.concatenate([y_mix, y_mem], axis=-1) @ w_out[layer]
        x = x + sqrelu_mlp(rms_norm(x, norm_mlp[layer]), w_mlp_in[layer], w_mlp_out[layer])
    return x
```

```python
import functools
import math

import jax
import jax.numpy as jnp
from jax import lax
from jax.experimental import pallas as pl
from jax.experimental.pallas import tpu as pltpu

EPS = 1e-6
NEG_INF = -1e30
LRU_C = 8.0
CONV_WIDTH = 4
SUBLANES = 8
LANES = 128
MXU_TILE = 256
VMEM_LIMIT_BYTES = 56 * 1024 * 1024

F32 = jnp.float32
BF16 = jnp.bfloat16


def _rms(x):
    return x * lax.rsqrt(jnp.mean(x * x, axis=-1, keepdims=True) + EPS)


def _dot(a, b):
    return jnp.dot(a, b, preferred_element_type=F32)


def _dot_nt(a, b):
    return lax.dot_general(a, b, (((1,), (1,)), ((), ())), preferred_element_type=F32)


def _sigmoid(x):
    return 1.0 / (1.0 + jnp.exp(-x))


def _softplus(x):
    return jnp.maximum(x, 0.0) + jnp.log(1.0 + jnp.exp(-jnp.abs(x)))


def _gelu_tanh(x):
    c = math.sqrt(2.0 / math.pi)
    return x * (0.5 * (1.0 + jnp.tanh(c * (x + 0.044715 * (x * x * x)))))


def _const_spec(shape):
    nd = len(shape)
    return pl.BlockSpec(shape, lambda *_: (0,) * nd)


def _mem_kv_kernel(mem_ref, g_ref, w_ref, kg_ref, k_ref, v_ref, *, heads, hd):
    width = heads * hd
    hn = (_rms(mem_ref[0]) * g_ref[0]).astype(BF16)
    kv = _dot(hn, w_ref[0])
    for h in range(heads):
        kh = kv[:, h * hd:(h + 1) * hd]
        k_ref[0, 0, :, h * hd:(h + 1) * hd] = (_rms(kh) * kg_ref[0]).astype(BF16)
    v_ref[0, 0] = kv[:, width:].astype(BF16)


def _mem_kv(mem, norm_mem, w_mem_kv, xa_k_gain):
    depth, d_model, two_w = w_mem_kv.shape
    b, m, _ = mem.shape
    width = two_w // 2
    hd = xa_k_gain.shape[-1]
    heads = width // hd
    out = jax.ShapeDtypeStruct((depth, b, m, width), BF16)
    return pl.pallas_call(
        functools.partial(_mem_kv_kernel, heads=heads, hd=hd),
        grid=(depth, b),
        in_specs=[
            pl.BlockSpec((1, m, d_model), lambda l, i: (i, 0, 0)),
            pl.BlockSpec((1, 1, d_model), lambda l, i: (l, 0, 0)),
            pl.BlockSpec((1, d_model, two_w), lambda l, i: (l, 0, 0)),
            pl.BlockSpec((1, 1, hd), lambda l, i: (l, 0, 0)),
        ],
        out_specs=[
            pl.BlockSpec((1, 1, m, width), lambda l, i: (l, i, 0, 0)),
            pl.BlockSpec((1, 1, m, width), lambda l, i: (l, i, 0, 0)),
        ],
        out_shape=[out, out],
        compiler_params=pltpu.CompilerParams(
            dimension_semantics=("arbitrary", "arbitrary"),
            vmem_limit_bytes=VMEM_LIMIT_BYTES),
        name="mem_kv",
    )(mem, norm_mem.reshape(depth, 1, d_model), w_mem_kv.astype(BF16),
      xa_k_gain.reshape(depth, 1, hd))


def _mem_attn(xq, k_ref, v_ref, qg, heads, hd):
    scale = 1.0 / math.sqrt(hd)
    outs = []
    for h in range(heads):
        q = (_rms(xq[:, h * hd:(h + 1) * hd]) * (qg * scale)).astype(BF16)
        s = _dot_nt(q, k_ref[0, :, h * hd:(h + 1) * hd])
        p = jnp.exp(s - jnp.max(s, axis=-1, keepdims=True))
        l = jnp.sum(p, axis=-1, keepdims=True)
        o = _dot(p.astype(BF16), v_ref[0, :, h * hd:(h + 1) * hd])
        outs.append(o / l)
    return jnp.concatenate(outs, axis=1)


def _lru_front_kernel(x_ref, g_ref, w_in_ref, cw_ref, cb_ref, wg_ref, br_ref, bi_ref,
                      lam_ref, k_ref, v_ref, qg_ref, y_ref, ym_ref,
                      u_ext, a_s, b_s, hc_ref, *, ts, d_lru, xa_heads, xa_hd):
    half = d_lru // 2

    @pl.when(pl.program_id(1) == 0)
    def _():
        u_ext[0:SUBLANES, :] = jnp.zeros((SUBLANES, d_lru), F32)
        hc_ref[...] = jnp.zeros_like(hc_ref)

    h = (_rms(x_ref[0]) * g_ref[...]).astype(BF16)
    proj = _dot(h, w_in_ref[...])
    u = proj[:, :d_lru]
    gate = proj[:, d_lru:2 * d_lru]
    xq = proj[:, 2 * d_lru:]

    u_ext[SUBLANES:SUBLANES + ts, :] = u
    uc = cb_ref[...]
    for k in range(CONV_WIDTH):
        off = SUBLANES - (CONV_WIDTH - 1) + k
        uc = uc + cw_ref[k:k + 1, :] * u_ext[off:off + ts, :]
    u_ext[0:SUBLANES, :] = u[ts - SUBLANES:, :]

    ub = uc.astype(BF16)
    g0 = _dot(ub[:, :half], wg_ref[0])
    g1 = _dot(ub[:, half:], wg_ref[1])
    r_pre = jnp.concatenate([g0[:, :half], g1[:, :half]], axis=1)
    i_pre = jnp.concatenate([g0[:, half:], g1[:, half:]], axis=1)
    r = _sigmoid(r_pre + br_ref[...])
    i = _sigmoid(i_pre + bi_ref[...])
    log_a = (-LRU_C) * r * _softplus(-lam_ref[...])
    a = jnp.exp(log_a)
    a_s[...] = a
    b_s[...] = jnp.sqrt(1.0 - a * a) * (i * uc)

    row = lax.broadcasted_iota(jnp.int32, (SUBLANES, d_lru), 0)

    def group(gidx, hc):
        r0 = pl.multiple_of(gidx * SUBLANES, SUBLANES)
        av = a_s[pl.ds(r0, SUBLANES), :]
        bv = b_s[pl.ds(r0, SUBLANES), :]
        for d in (1, 2, 4):
            keep = row >= d
            a_sh = jnp.where(keep, pltpu.roll(av, d, 0), 1.0)
            b_sh = jnp.where(keep, pltpu.roll(bv, d, 0), 0.0)
            bv = av * b_sh + bv
            av = av * a_sh
        hv = bv + av * hc
        b_s[pl.ds(r0, SUBLANES), :] = hv
        return hv[SUBLANES - 1:SUBLANES, :]

    hc_ref[...] = lax.fori_loop(0, ts // SUBLANES, group, hc_ref[...], unroll=8)

    y_ref[0] = (b_s[...] * _gelu_tanh(gate)).astype(BF16)
    ym_ref[0] = _mem_attn(xq, k_ref, v_ref, qg_ref[...], xa_heads, xa_hd).astype(BF16)


def _lru_front(x, norm_g, w_in, conv_w, conv_b, w_r, b_r, w_i, b_i, lam, k_mem, v_mem, xa_qg, ts):
    b, s, d_model = x.shape
    d_lru = conv_w.shape[-1]
    half = d_lru // 2
    xa_hd = xa_qg.shape[-1]
    xa_width = w_in.shape[1] - 2 * d_lru
    xa_heads = xa_width // xa_hd
    m = k_mem.shape[1]
    blk = w_r.shape[-1]
    nblk_half = half // blk
    wg = jnp.stack([
        jnp.concatenate([jax.scipy.linalg.block_diag(*w_r[hf * nblk_half:(hf + 1) * nblk_half]),
                         jax.scipy.linalg.block_diag(*w_i[hf * nblk_half:(hf + 1) * nblk_half])], axis=1)
        for hf in range(2)]).astype(BF16)
    row = lambda v: v.reshape(1, -1)
    out = jax.ShapeDtypeStruct((b, s, d_lru), BF16)
    outm = jax.ShapeDtypeStruct((b, s, xa_width), BF16)
    return pl.pallas_call(
        functools.partial(_lru_front_kernel, ts=ts, d_lru=d_lru, xa_heads=xa_heads, xa_hd=xa_hd),
        grid=(b, s // ts),
        in_specs=[
            pl.BlockSpec((1, ts, d_model), lambda i, j: (i, j, 0)),
            _const_spec((1, d_model)),
            _const_spec(w_in.shape),
            _const_spec(conv_w.shape),
            _const_spec((1, d_lru)),
            _const_spec(wg.shape),
            _const_spec((1, d_lru)),
            _const_spec((1, d_lru)),
            _const_spec((1, d_lru)),
            pl.BlockSpec((1, m, xa_width), lambda i, j: (i, 0, 0)),
            pl.BlockSpec((1, m, xa_width), lambda i, j: (i, 0, 0)),
            _const_spec((1, xa_hd)),
        ],
        out_specs=[
            pl.BlockSpec((1, ts, d_lru), lambda i, j: (i, j, 0)),
            pl.BlockSpec((1, ts, xa_width), lambda i, j: (i, j, 0)),
        ],
        out_shape=[out, outm],
        scratch_shapes=[
            pltpu.VMEM((ts + SUBLANES, d_lru), F32),
            pltpu.VMEM((ts, d_lru), F32),
            pltpu.VMEM((ts, d_lru), F32),
            pltpu.VMEM((1, d_lru), F32),
        ],
        compiler_params=pltpu.CompilerParams(
            dimension_semantics=("arbitrary", "arbitrary"),
            vmem_limit_bytes=VMEM_LIMIT_BYTES),
        name="lru_front",
    )(x, row(norm_g), w_in.astype(BF16), conv_w, row(conv_b), wg, row(b_r), row(b_i), row(lam),
      k_mem, v_mem, row(xa_qg))


def _fox_front_kernel(x_ref, g_ref, w_ref, hm_ref, qg_ref, kg_ref, bf_ref, k_ref, v_ref, xqg_ref,
                      q_out, k_out, v_out, c_out, ym_ref, c_carry,
                      *, ts, width, heads, xa_heads, xa_hd):
    @pl.when(pl.program_id(1) == 0)
    def _():
        c_carry[...] = jnp.zeros_like(c_carry)

    h = (_rms(x_ref[0]) * g_ref[...]).astype(BF16)
    proj = _dot(h, w_ref[...])
    q = proj[:, :width]
    k = proj[:, width:2 * width]
    v_out[0] = proj[:, 2 * width:3 * width].astype(BF16)
    xq = proj[:, 3 * width:3 * width + xa_heads * xa_hd]
    f = proj[:, 3 * width + xa_heads * xa_hd:]

    def head_ms(t):
        t2 = (t * t).astype(BF16)
        return jnp.concatenate(
            [_dot(t2[:, c:c + MXU_TILE], hm_ref[...]) for c in range(0, width, MXU_TILE)], axis=1)

    q_out[0] = (q * lax.rsqrt(head_ms(q) + EPS) * qg_ref[...]).astype(BF16)
    k_out[0] = (k * lax.rsqrt(head_ms(k) + EPS) * kg_ref[...]).astype(BF16)

    ft = jnp.transpose(f)[:heads, :] + bf_ref[...]
    c = -_softplus(-ft)
    lane = lax.broadcasted_iota(jnp.int32, (heads, ts), 1)
    d = 1
    while d < ts:
        c = c + jnp.where(lane >= d, pltpu.roll(c, d, 1), 0.0)
        d *= 2
    c = c + c_carry[:, 0:1]
    c_out[0] = c
    c_carry[...] = jnp.broadcast_to(c[:, ts - 1:ts], c_carry.shape)

    ym_ref[0] = _mem_attn(xq, k_ref, v_ref, xqg_ref[...], xa_heads, xa_hd).astype(BF16)


def _fox_front(x, norm_g, w_in, b_f, q_gain, k_gain, k_mem, v_mem, xa_qg, ts):
    b, s, d_model = x.shape
    hd = q_gain.shape[-1]
    heads = b_f.shape[-1]
    width = heads * hd
    xa_hd = xa_qg.shape[-1]
    xa_width = w_in.shape[1] - 3 * width - heads
    xa_heads = xa_width // xa_hd
    m = k_mem.shape[1]
    w_f = w_in[:, 3 * width:3 * width + heads]
    w_all = jnp.concatenate(
        [w_in[:, :3 * width], w_in[:, 3 * width + heads:], w_f,
         jnp.zeros((d_model, LANES - heads), w_in.dtype)], axis=1).astype(BF16)
    hid = jnp.arange(MXU_TILE) // hd
    head_mean = jnp.where(hid[:, None] == hid[None, :], 1.0 / hd, 0.0).astype(BF16)
    scale = 1.0 / math.sqrt(hd)
    qg = jnp.tile(q_gain * scale, heads).reshape(1, width)
    kg = jnp.tile(k_gain, heads).reshape(1, width)
    row = lambda t: t.reshape(1, -1)
    act = jax.ShapeDtypeStruct((b, s, width), BF16)
    return pl.pallas_call(
        functools.partial(_fox_front_kernel, ts=ts, width=width, heads=heads,
                          xa_heads=xa_heads, xa_hd=xa_hd),
        grid=(b, s // ts),
        in_specs=[
            pl.BlockSpec((1, ts, d_model), lambda i, j: (i, j, 0)),
            _const_spec((1, d_model)),
            _const_spec(w_all.shape),
            _const_spec(head_mean.shape),
            _const_spec((1, width)),
            _const_spec((1, width)),
            _const_spec((heads, 1)),
            pl.BlockSpec((1, m, xa_width), lambda i, j: (i, 0, 0)),
            pl.BlockSpec((1, m, xa_width), lambda i, j: (i, 0, 0)),
            _const_spec((1, xa_hd)),
        ],
        out_specs=[
            pl.BlockSpec((1, ts, width), lambda i, j: (i, j, 0)),
            pl.BlockSpec((1, ts, width), lambda i, j: (i, j, 0)),
            pl.BlockSpec((1, ts, width), lambda i, j: (i, j, 0)),
            pl.BlockSpec((1, heads, ts), lambda i, j: (i, 0, j)),
            pl.BlockSpec((1, ts, xa_width), lambda i, j: (i, j, 0)),
        ],
        out_shape=[act, act, act, jax.ShapeDtypeStruct((b, heads, s), F32),
                   jax.ShapeDtypeStruct((b, s, xa_width), BF16)],
        scratch_shapes=[pltpu.VMEM((heads, LANES), F32)],
        compiler_params=pltpu.CompilerParams(
            dimension_semantics=("arbitrary", "arbitrary"),
            vmem_limit_bytes=VMEM_LIMIT_BYTES),
        name="fox_front",
    )(x, row(norm_g), w_all, head_mean, qg, kg, b_f.reshape(heads, 1), k_mem, v_mem, row(xa_qg))


def _fox_attn_kernel(q_ref, k_ref, v_ref, c_ref, o_ref, *, tq, hd):
    i = pl.program_id(2)
    q = q_ref[0]
    lane = lax.broadcasted_iota(jnp.int32, (1, 2 * hd), 1)
    first = lane < hd
    zq = jnp.zeros_like(q)
    q0 = jnp.where(first, q, zq)
    q1 = jnp.where(first, zq, q)
    rows = lax.broadcasted_iota(jnp.int32, (tq, tq), 0)
    cols = lax.broadcasted_iota(jnp.int32, (tq, tq), 1)
    causal = cols <= rows

    def block(j, carry, masked):
        m0, l0, m1, l1, acc = carry
        c0 = pl.multiple_of(j * tq, tq)
        kb = k_ref[0, pl.ds(c0, tq), :]
        vb = v_ref[0, pl.ds(c0, tq), :]
        s0 = _dot_nt(q0, kb) - c_ref[0, 0, 0:1, pl.ds(c0, tq)]
        s1 = _dot_nt(q1, kb) - c_ref[0, 0, 1:2, pl.ds(c0, tq)]
        if masked:
            s0 = jnp.where(causal, s0, NEG_INF)
            s1 = jnp.where(causal, s1, NEG_INF)
        m0n = jnp.maximum(m0, jnp.max(s0, axis=-1, keepdims=True))
        m1n = jnp.maximum(m1, jnp.max(s1, axis=-1, keepdims=True))
        p0 = jnp.exp(s0 - m0n)
        p1 = jnp.exp(s1 - m1n)
        al0 = jnp.exp(m0 - m0n)
        al1 = jnp.exp(m1 - m1n)
        l0 = al0 * l0 + jnp.sum(p0, axis=-1, keepdims=True)
        l1 = al1 * l1 + jnp.sum(p1, axis=-1, keepdims=True)
        zv = jnp.zeros_like(vb)
        pv = _dot(p0.astype(BF16), jnp.where(first, vb, zv)) + _dot(p1.astype(BF16), jnp.where(first, zv, vb))
        acc = jnp.where(first, al0, al1) * acc + pv
        return m0n, l0, m1n, l1, acc

    col1 = lambda val: jnp.full((tq, 1), val, F32)
    init = (col1(NEG_INF), col1(0.0), col1(NEG_INF), col1(0.0), jnp.zeros((tq, 2 * hd), F32))
    carry = lax.fori_loop(0, i, functools.partial(block, masked=False), init)
    _, l0, _, l1, acc = block(i, carry, True)
    o_ref[0] = (acc / jnp.where(first, l0, l1)).astype(BF16)


def _fox_attn(q, k, v, c, hd, tq):
    b, s, width = q.shape
    pairs = width // (2 * hd)
    c4 = c.reshape(b, pairs, 2, s)
    return pl.pallas_call(
        functools.partial(_fox_attn_kernel, tq=tq, hd=hd),
        grid=(b, pairs, s // tq),
        in_specs=[
            pl.BlockSpec((1, tq, 2 * hd), lambda bi, p, i: (bi, i, p)),
            pl.BlockSpec((1, s, 2 * hd), lambda bi, p, i: (bi, 0, p)),
            pl.BlockSpec((1, s, 2 * hd), lambda bi, p, i: (bi, 0, p)),
            pl.BlockSpec((1, 1, 2, s), lambda bi, p, i: (bi, p, 0, 0)),
        ],
        out_specs=pl.BlockSpec((1, tq, 2 * hd), lambda bi, p, i: (bi, i, p)),
        out_shape=jax.ShapeDtypeStruct((b, s, width), BF16),
        compiler_params=pltpu.CompilerParams(
            dimension_semantics=("arbitrary", "arbitrary", "arbitrary"),
            vmem_limit_bytes=VMEM_LIMIT_BYTES),
        name="fox_attn",
    )(q, k, v, c4)


def _out_mlp_kernel(x_ref, ya_ref, yb_ref, wo_ref, g_ref, w1_ref, w2_ref, o_ref, *, wa, ff_chunk):
    x1 = x_ref[...] + _dot(ya_ref[...], wo_ref[:wa, :]) + _dot(yb_ref[...], wo_ref[wa:, :])
    h = (_rms(x1) * g_ref[...]).astype(BF16)
    d_ff = w1_ref.shape[1]
    o_ref[...] = x1
    for c0 in range(0, d_ff, ff_chunk):
        a = jnp.maximum(_dot(h, w1_ref[:, c0:c0 + ff_chunk]), 0.0)
        o_ref[...] += _dot((a * a).astype(BF16), w2_ref[c0:c0 + ff_chunk, :])


def _out_mlp(x, ya, yb, w_out, norm_g, w1, w2, tm):
    b, s, d_model = x.shape
    t = b * s
    wa = ya.shape[-1]
    wb = yb.shape[-1]
    d_ff = w1.shape[1]
    out = pl.pallas_call(
        functools.partial(_out_mlp_kernel, wa=wa, ff_chunk=min(d_ff, 1024)),
        grid=(t // tm,),
        in_specs=[
            pl.BlockSpec((tm, d_model), lambda i: (i, 0)),
            pl.BlockSpec((tm, wa), lambda i: (i, 0)),
            pl.BlockSpec((tm, wb), lambda i: (i, 0)),
            _const_spec(w_out.shape),
            _const_spec((1, d_model)),
            _const_spec(w1.shape),
            _const_spec(w2.shape),
        ],
        out_specs=pl.BlockSpec((tm, d_model), lambda i: (i, 0)),
        out_shape=jax.ShapeDtypeStruct((t, d_model), F32),
        compiler_params=pltpu.CompilerParams(
            dimension_semantics=("arbitrary",),
            vmem_limit_bytes=VMEM_LIMIT_BYTES),
        name="out_mlp",
    )(x.reshape(t, d_model), ya.reshape(t, wa), yb.reshape(t, wb), w_out.astype(BF16),
      norm_g.reshape(1, d_model), w1.astype(BF16), w2.astype(BF16))
    return out.reshape(b, s, d_model)


def kernel(x, mem, norm_mix, norm_mem, w_mem_kv, xa_q_gain, xa_k_gain, w_out, norm_mlp, w_mlp_in, w_mlp_out, w_in_a, conv_w, conv_b, w_rgate, b_rgate, w_igate, b_igate, lru_lambda, w_in_b, b_forget, fox_q_gain, fox_k_gain):
    depth = norm_mix.shape[0]
    s = x.shape[1]
    ts = min(s, 512)
    tq = min(s, 256)
    k_mem, v_mem = _mem_kv(mem, norm_mem, w_mem_kv, xa_k_gain)
    for layer in range(depth):
        j = layer // 2
        if layer % 2 == 0:
            ya, yb = _lru_front(x, norm_mix[layer], w_in_a[j], conv_w[j], conv_b[j], w_rgate[j], b_rgate[j],
                                w_igate[j], b_igate[j], lru_lambda[j], k_mem[layer], v_mem[layer],
                                xa_q_gain[layer], ts)
        else:
            q, k, v, c, yb = _fox_front(x, norm_mix[layer], w_in_b[j], b_forget[j], fox_q_gain[j],
                                        fox_k_gain[j], k_mem[layer], v_mem[layer], xa_q_gain[layer], ts)
            ya = _fox_attn(q, k, v, c, fox_q_gain.shape[-1], tq)
        x = _out_mlp(x, ya, yb, w_out[layer], norm_mlp[layer], w_mlp_in[layer], w_mlp_out[layer], ts)
    return x
```

```python
import functools
import math

import jax
import jax.numpy as jnp
from jax import lax
from jax.experimental import pallas as pl
from jax.experimental.pallas import tpu as pltpu

EPS = 1e-6
NEG_INF = -1e30
LRU_C = 8.0
CONV_WIDTH = 4
LOG2E = math.log2(math.e)
SUBLANES = 8
LANES = 128
MXU_TILE = 256
VMEM_LIMIT_BYTES = 56 * 1024 * 1024

F32 = jnp.float32
BF16 = jnp.bfloat16


def _rms(x):
    return x * lax.rsqrt(jnp.mean(x * x, axis=-1, keepdims=True) + EPS)


def _dot(a, b):
    return jnp.dot(a, b, preferred_element_type=F32)


def _dot_nt(a, b):
    return lax.dot_general(a, b, (((1,), (1,)), ((), ())), preferred_element_type=F32)


def _sigmoid(x):
    return 1.0 / (1.0 + jnp.exp(-x))


def _softplus(x):
    return jnp.maximum(x, 0.0) + jnp.log(1.0 + jnp.exp(-jnp.abs(x)))


def _gelu_tanh(x):
    c = math.sqrt(2.0 / math.pi)
    return x * (0.5 * (1.0 + jnp.tanh(c * (x + 0.044715 * (x * x * x)))))


def _const_spec(shape):
    nd = len(shape)
    return pl.BlockSpec(shape, lambda *_: (0,) * nd)


def _mem_kv_kernel(mem_ref, g_ref, w_ref, kg_ref, k_ref, v_ref, *, heads, hd):
    width = heads * hd
    hn = (_rms(mem_ref[0]) * g_ref[0]).astype(BF16)
    kv = _dot(hn, w_ref[0])
    for h in range(heads):
        kh = kv[:, h * hd:(h + 1) * hd]
        k_ref[0, 0, :, h * hd:(h + 1) * hd] = (_rms(kh) * kg_ref[0]).astype(BF16)
    v_ref[0, 0] = kv[:, width:].astype(BF16)


def _mem_kv(mem, norm_mem, w_mem_kv, xa_k_gain):
    depth, d_model, two_w = w_mem_kv.shape
    b, m, _ = mem.shape
    width = two_w // 2
    hd = xa_k_gain.shape[-1]
    heads = width // hd
    out = jax.ShapeDtypeStruct((depth, b, m, width), BF16)
    return pl.pallas_call(
        functools.partial(_mem_kv_kernel, heads=heads, hd=hd),
        grid=(depth, b),
        in_specs=[
            pl.BlockSpec((1, m, d_model), lambda l, i: (i, 0, 0)),
            pl.BlockSpec((1, 1, d_model), lambda l, i: (l, 0, 0)),
            pl.BlockSpec((1, d_model, two_w), lambda l, i: (l, 0, 0)),
            pl.BlockSpec((1, 1, hd), lambda l, i: (l, 0, 0)),
        ],
        out_specs=[
            pl.BlockSpec((1, 1, m, width), lambda l, i: (l, i, 0, 0)),
            pl.BlockSpec((1, 1, m, width), lambda l, i: (l, i, 0, 0)),
        ],
        out_shape=[out, out],
        compiler_params=pltpu.CompilerParams(
            dimension_semantics=("arbitrary", "arbitrary"),
            vmem_limit_bytes=VMEM_LIMIT_BYTES),
        name="mem_kv",
    )(mem, norm_mem.reshape(depth, 1, d_model), w_mem_kv.astype(BF16),
      xa_k_gain.reshape(depth, 1, hd))


def _mem_attn(xq, k_ref, v_ref, qg, heads, hd):
    scale = 1.0 / math.sqrt(hd)
    outs = []
    for h in range(heads):
        q = (_rms(xq[:, h * hd:(h + 1) * hd]) * (qg * scale)).astype(BF16)
        s = _dot_nt(q, k_ref[0, :, h * hd:(h + 1) * hd])
        p = jnp.exp(s - jnp.max(s, axis=-1, keepdims=True))
        l = jnp.sum(p, axis=-1, keepdims=True)
        o = _dot(p.astype(BF16), v_ref[0, :, h * hd:(h + 1) * hd])
        outs.append(o / l)
    return jnp.concatenate(outs, axis=1)


def _lru_front_kernel(x_ref, g_ref, w_in_ref, cw_ref, cb_ref, wg_ref, br_ref, bi_ref,
                      lam_ref, k_ref, v_ref, qg_ref, y_ref, ym_ref,
                      u_ext, a_s, b_s, hc_ref, *, ts, d_lru, xa_heads, xa_hd):
    half = d_lru // 2

    @pl.when(pl.program_id(1) == 0)
    def _():
        u_ext[0:SUBLANES, :] = jnp.zeros((SUBLANES, d_lru), F32)
        hc_ref[...] = jnp.zeros_like(hc_ref)

    h = (_rms(x_ref[0]) * g_ref[...]).astype(BF16)
    proj = _dot(h, w_in_ref[...])
    u = proj[:, :d_lru]
    gate = proj[:, d_lru:2 * d_lru]
    xq = proj[:, 2 * d_lru:]

    u_ext[SUBLANES:SUBLANES + ts, :] = u
    uc = cb_ref[...]
    for k in range(CONV_WIDTH):
        off = SUBLANES - (CONV_WIDTH - 1) + k
        uc = uc + cw_ref[k:k + 1, :] * u_ext[off:off + ts, :]
    u_ext[0:SUBLANES, :] = u[ts - SUBLANES:, :]

    ub = uc.astype(BF16)
    g0 = _dot(ub[:, :half], wg_ref[0])
    g1 = _dot(ub[:, half:], wg_ref[1])
    r_pre = jnp.concatenate([g0[:, :half], g1[:, :half]], axis=1)
    i_pre = jnp.concatenate([g0[:, half:], g1[:, half:]], axis=1)
    r = _sigmoid(r_pre + br_ref[...])
    i = _sigmoid(i_pre + bi_ref[...])
    log_a = (-LRU_C) * r * _softplus(-lam_ref[...])
    a = jnp.exp(log_a)
    a_s[...] = a
    b_s[...] = jnp.sqrt(1.0 - a * a) * (i * uc)

    row = lax.broadcasted_iota(jnp.int32, (SUBLANES, d_lru), 0)

    def group(gidx, hc):
        r0 = pl.multiple_of(gidx * SUBLANES, SUBLANES)
        av = a_s[pl.ds(r0, SUBLANES), :]
        bv = b_s[pl.ds(r0, SUBLANES), :]
        for d in (1, 2, 4):
            keep = row >= d
            a_sh = jnp.where(keep, pltpu.roll(av, d, 0), 1.0)
            b_sh = jnp.where(keep, pltpu.roll(bv, d, 0), 0.0)
            bv = av * b_sh + bv
            av = av * a_sh
        hv = bv + av * hc
        b_s[pl.ds(r0, SUBLANES), :] = hv
        return hv[SUBLANES - 1:SUBLANES, :]

    hc_ref[...] = lax.fori_loop(0, ts // SUBLANES, group, hc_ref[...], unroll=8)

    y_ref[0] = (b_s[...] * _gelu_tanh(gate)).astype(BF16)
    ym_ref[0] = _mem_attn(xq, k_ref, v_ref, qg_ref[...], xa_heads, xa_hd).astype(BF16)


def _lru_front(x, norm_g, w_in, conv_w, conv_b, w_r, b_r, w_i, b_i, lam, k_mem, v_mem, xa_qg, ts):
    b, s, d_model = x.shape
    d_lru = conv_w.shape[-1]
    half = d_lru // 2
    xa_hd = xa_qg.shape[-1]
    xa_width = w_in.shape[1] - 2 * d_lru
    xa_heads = xa_width // xa_hd
    m = k_mem.shape[1]
    blk = w_r.shape[-1]
    nblk_half = half // blk
    wg = jnp.stack([
        jnp.concatenate([jax.scipy.linalg.block_diag(*w_r[hf * nblk_half:(hf + 1) * nblk_half]),
                         jax.scipy.linalg.block_diag(*w_i[hf * nblk_half:(hf + 1) * nblk_half])], axis=1)
        for hf in range(2)]).astype(BF16)
    row = lambda v: v.reshape(1, -1)
    out = jax.ShapeDtypeStruct((b, s, d_lru), BF16)
    outm = jax.ShapeDtypeStruct((b, s, xa_width), BF16)
    return pl.pallas_call(
        functools.partial(_lru_front_kernel, ts=ts, d_lru=d_lru, xa_heads=xa_heads, xa_hd=xa_hd),
        grid=(b, s // ts),
        in_specs=[
            pl.BlockSpec((1, ts, d_model), lambda i, j: (i, j, 0)),
            _const_spec((1, d_model)),
            _const_spec(w_in.shape),
            _const_spec(conv_w.shape),
            _const_spec((1, d_lru)),
            _const_spec(wg.shape),
            _const_spec((1, d_lru)),
            _const_spec((1, d_lru)),
            _const_spec((1, d_lru)),
            pl.BlockSpec((1, m, xa_width), lambda i, j: (i, 0, 0)),
            pl.BlockSpec((1, m, xa_width), lambda i, j: (i, 0, 0)),
            _const_spec((1, xa_hd)),
        ],
        out_specs=[
            pl.BlockSpec((1, ts, d_lru), lambda i, j: (i, j, 0)),
            pl.BlockSpec((1, ts, xa_width), lambda i, j: (i, j, 0)),
        ],
        out_shape=[out, outm],
        scratch_shapes=[
            pltpu.VMEM((ts + SUBLANES, d_lru), F32),
            pltpu.VMEM((ts, d_lru), F32),
            pltpu.VMEM((ts, d_lru), F32),
            pltpu.VMEM((1, d_lru), F32),
        ],
        compiler_params=pltpu.CompilerParams(
            dimension_semantics=("arbitrary", "arbitrary"),
            vmem_limit_bytes=VMEM_LIMIT_BYTES),
        name="lru_front",
    )(x, row(norm_g), w_in.astype(BF16), conv_w, row(conv_b), wg, row(b_r), row(b_i), row(lam),
      k_mem, v_mem, row(xa_qg))


def _fox_front_kernel(x_ref, g_ref, w_ref, hm_ref, qg_ref, kg_ref, bf_ref, sel_ref, k_ref, v_ref,
                      xqg_ref, qt_out, ka_out, vt_out, ym_ref, c_carry,
                      *, ts, width, xa_heads, xa_hd):
    @pl.when(pl.program_id(1) == 0)
    def _():
        c_carry[...] = jnp.zeros_like(c_carry)

    h = (_rms(x_ref[0]) * g_ref[...]).astype(BF16)
    proj = _dot(h, w_ref[...])
    q = proj[:, :width]
    k = proj[:, width:2 * width]
    v = proj[:, 2 * width:3 * width]
    xq = proj[:, 3 * width:3 * width + xa_heads * xa_hd]
    f = proj[:, 3 * width + xa_heads * xa_hd:]

    def head_ms(t):
        t2 = (t * t).astype(BF16)
        return jnp.concatenate(
            [_dot(t2[:, c:c + MXU_TILE], hm_ref[...]) for c in range(0, width, MXU_TILE)], axis=1)

    qt_out[0] = jnp.transpose(q * lax.rsqrt(head_ms(q) + EPS) * qg_ref[...]).astype(BF16)
    vt_out[0] = jnp.transpose(v).astype(BF16)
    kn = (k * lax.rsqrt(head_ms(k) + EPS) * kg_ref[...]).astype(BF16)

    c = -_softplus(-(f + bf_ref[...]))
    row = lax.broadcasted_iota(jnp.int32, c.shape, 0)
    d = 1
    while d < ts:
        c = c + jnp.where(row >= d, pltpu.roll(c, d, 0), 0.0)
        d *= 2
    c = c + c_carry[...]
    c_carry[...] = c[ts - 1:ts, :]

    b2 = c * (-LOG2E)
    hi = b2.astype(BF16)
    r1 = b2 - hi.astype(F32)
    mid = r1.astype(BF16)
    lo = (r1 - mid.astype(F32)).astype(BF16)
    bias = (_dot(hi, sel_ref[0]) + _dot(mid, sel_ref[1]) + _dot(lo, sel_ref[2])).astype(BF16)
    for p in range(width // LANES):
        ka_out[0, :, 2 * LANES * p:2 * LANES * p + LANES] = kn[:, LANES * p:LANES * (p + 1)]
        ka_out[0, :, 2 * LANES * p + LANES:2 * LANES * (p + 1)] = bias[:, LANES * p:LANES * (p + 1)]

    ym_ref[0] = _mem_attn(xq, k_ref, v_ref, xqg_ref[...], xa_heads, xa_hd).astype(BF16)


def _fox_front(x, norm_g, w_in, b_f, q_gain, k_gain, k_mem, v_mem, xa_qg, ts):
    b, s, d_model = x.shape
    hd = q_gain.shape[-1]
    heads = b_f.shape[-1]
    width = heads * hd
    pairs = width // LANES
    xa_hd = xa_qg.shape[-1]
    xa_width = w_in.shape[1] - 3 * width - heads
    xa_heads = xa_width // xa_hd
    m = k_mem.shape[1]
    w_f = w_in[:, 3 * width:3 * width + heads]
    w_all = jnp.concatenate(
        [w_in[:, :3 * width], w_in[:, 3 * width + heads:], w_f,
         jnp.zeros((d_model, LANES - heads), w_in.dtype)], axis=1).astype(BF16)
    hid = jnp.arange(MXU_TILE) // hd
    head_mean = jnp.where(hid[:, None] == hid[None, :], 1.0 / hd, 0.0).astype(BF16)
    src = jnp.arange(LANES)[None, :, None]
    dst = jnp.arange(pairs * LANES)[None, None, :]
    piece = jnp.arange(3)[:, None, None]
    sel = ((src < heads) & (dst == LANES * (src // 2) + 3 * (src % 2) + piece)).astype(BF16)
    qg = jnp.tile(q_gain * (LOG2E / math.sqrt(hd)), heads).reshape(1, width)
    kg = jnp.tile(k_gain, heads).reshape(1, width)
    bfp = jnp.zeros((1, LANES), F32).at[0, :heads].set(b_f)
    row = lambda t: t.reshape(1, -1)
    act_t = jax.ShapeDtypeStruct((b, width, s), BF16)
    return pl.pallas_call(
        functools.partial(_fox_front_kernel, ts=ts, width=width, xa_heads=xa_heads, xa_hd=xa_hd),
        grid=(b, s // ts),
        in_specs=[
            pl.BlockSpec((1, ts, d_model), lambda i, j: (i, j, 0)),
            _const_spec((1, d_model)),
            _const_spec(w_all.shape),
            _const_spec(head_mean.shape),
            _const_spec((1, width)),
            _const_spec((1, width)),
            _const_spec((1, LANES)),
            _const_spec(sel.shape),
            pl.BlockSpec((1, m, xa_width), lambda i, j: (i, 0, 0)),
            pl.BlockSpec((1, m, xa_width), lambda i, j: (i, 0, 0)),
            _const_spec((1, xa_hd)),
        ],
        out_specs=[
            pl.BlockSpec((1, width, ts), lambda i, j: (i, 0, j)),
            pl.BlockSpec((1, ts, 2 * width), lambda i, j: (i, j, 0)),
            pl.BlockSpec((1, width, ts), lambda i, j: (i, 0, j)),
            pl.BlockSpec((1, ts, xa_width), lambda i, j: (i, j, 0)),
        ],
        out_shape=[act_t, jax.ShapeDtypeStruct((b, s, 2 * width), BF16), act_t,
                   jax.ShapeDtypeStruct((b, s, xa_width), BF16)],
        scratch_shapes=[pltpu.VMEM((1, LANES), F32)],
        compiler_params=pltpu.CompilerParams(
            dimension_semantics=("arbitrary", "arbitrary"),
            vmem_limit_bytes=VMEM_LIMIT_BYTES),
        name="fox_front",
    )(x, row(norm_g), w_all, head_mean, qg, kg, bfp, sel, k_mem, v_mem, row(xa_qg))


def _fox_attn_kernel(qt_ref, ka_ref, vt_ref, o_ref, s0_ref, s1_ref, m_ref, l_ref, acc_ref, *, t, hd):
    i = pl.program_id(2)
    qt = qt_ref[0]
    rowq = lax.broadcasted_iota(jnp.int32, (2 * hd, t), 0)
    keys = lax.broadcasted_iota(jnp.int32, (t, t), 0)
    queries = lax.broadcasted_iota(jnp.int32, (t, t), 1)
    causal = keys <= queries

    def weights(slot):
        top = jnp.where((rowq >= hd * slot) & (rowq < hd * (slot + 1)), qt, jnp.zeros_like(qt))
        bot = jnp.where((rowq >= 3 * slot) & (rowq < 3 * slot + 3), 1.0, 0.0).astype(BF16)
        return jnp.concatenate([top, bot], axis=0)

    w = [weights(0), weights(1)]

    def logits(j, s_ref):
        ka = ka_ref[0, pl.ds(pl.multiple_of(j * t, t), t), :]
        for slot in range(2):
            s_ref[slot] = _dot(ka, w[slot])

    def softmax_pv(j, s_ref, masked):
        c0 = pl.multiple_of(j * t, t)
        for slot in range(2):
            st = s_ref[slot]
            if masked:
                st = jnp.where(causal, st, NEG_INF)
            m = m_ref[slot]
            mn = jnp.maximum(m, jnp.max(st, axis=0, keepdims=True))
            pt = jnp.exp2(st - mn)
            al = jnp.exp2(m - mn)
            m_ref[slot] = mn
            l_ref[slot] = al * l_ref[slot] + jnp.sum(pt, axis=0, keepdims=True)
            vt = vt_ref[0, hd * slot:hd * (slot + 1), pl.ds(c0, t)]
            acc_ref[slot] = al * acc_ref[slot] + _dot(vt, pt.astype(BF16))

    m_ref[...] = jnp.full(m_ref.shape, NEG_INF, F32)
    l_ref[...] = jnp.zeros_like(l_ref)
    acc_ref[...] = jnp.zeros_like(acc_ref)

    logits(0, s0_ref)

    def two_blocks(jj, carry):
        j = 2 * jj
        logits(j + 1, s1_ref)
        softmax_pv(j, s0_ref, False)
        logits(j + 2, s0_ref)
        softmax_pv(j + 1, s1_ref, False)
        return carry

    lax.fori_loop(0, lax.shift_right_logical(i, 1), two_blocks, 0)

    @pl.when(lax.bitwise_and(i, 1) == 0)
    def _():
        softmax_pv(i, s0_ref, True)

    @pl.when(lax.bitwise_and(i, 1) == 1)
    def _():
        logits(i, s1_ref)
        softmax_pv(i - 1, s0_ref, False)
        softmax_pv(i, s1_ref, True)

    for slot in range(2):
        o_ref[0, hd * slot:hd * (slot + 1), :] = (acc_ref[slot] / l_ref[slot]).astype(BF16)


def _fox_attn(qt, kaug, vt, hd, t):
    b, width, s = qt.shape
    pairs = width // (2 * hd)
    return pl.pallas_call(
        functools.partial(_fox_attn_kernel, t=t, hd=hd),
        grid=(b, pairs, s // t),
        in_specs=[
            pl.BlockSpec((1, 2 * hd, t), lambda bi, p, i: (bi, p, i)),
            pl.BlockSpec((1, s, 4 * hd), lambda bi, p, i: (bi, 0, p)),
            pl.BlockSpec((1, 2 * hd, s), lambda bi, p, i: (bi, p, 0)),
        ],
        out_specs=pl.BlockSpec((1, 2 * hd, t), lambda bi, p, i: (bi, p, i)),
        out_shape=jax.ShapeDtypeStruct((b, width, s), BF16),
        scratch_shapes=[
            pltpu.VMEM((2, t, t), F32),
            pltpu.VMEM((2, t, t), F32),
            pltpu.VMEM((2, 1, t), F32),
            pltpu.VMEM((2, 1, t), F32),
            pltpu.VMEM((2, hd, t), F32),
        ],
        compiler_params=pltpu.CompilerParams(
            dimension_semantics=("arbitrary", "arbitrary", "arbitrary"),
            vmem_limit_bytes=VMEM_LIMIT_BYTES),
        name="fox_attn",
    )(qt, kaug, vt)


def _out_mlp_kernel(x_ref, ya_ref, yb_ref, wo_ref, g_ref, w1_ref, w2_ref, o_ref,
                    *, wa, ff_chunk, ya_transposed):
    if ya_transposed:
        mix = lax.dot_general(ya_ref[0], wo_ref[:wa, :], (((0,), (0,)), ((), ())),
                              preferred_element_type=F32)
    else:
        mix = _dot(ya_ref[0], wo_ref[:wa, :])
    x1 = x_ref[0] + mix + _dot(yb_ref[0], wo_ref[wa:, :])
    h = (_rms(x1) * g_ref[...]).astype(BF16)
    d_ff = w1_ref.shape[1]
    o_ref[0] = x1
    for c0 in range(0, d_ff, ff_chunk):
        a = jnp.maximum(_dot(h, w1_ref[:, c0:c0 + ff_chunk]), 0.0)
        o_ref[0] += _dot((a * a).astype(BF16), w2_ref[c0:c0 + ff_chunk, :])


def _out_mlp(x, ya, yb, w_out, norm_g, w1, w2, tm, ya_transposed):
    b, s, d_model = x.shape
    wb = yb.shape[-1]
    wa = w_out.shape[0] - wb
    d_ff = w1.shape[1]
    if ya_transposed:
        ya_spec = pl.BlockSpec((1, wa, tm), lambda i, j: (i, 0, j))
    else:
        ya_spec = pl.BlockSpec((1, tm, wa), lambda i, j: (i, j, 0))
    return pl.pallas_call(
        functools.partial(_out_mlp_kernel, wa=wa, ff_chunk=min(d_ff, 1024), ya_transposed=ya_transposed),
        grid=(b, s // tm),
        in_specs=[
            pl.BlockSpec((1, tm, d_model), lambda i, j: (i, j, 0)),
            ya_spec,
            pl.BlockSpec((1, tm, wb), lambda i, j: (i, j, 0)),
            _const_spec(w_out.shape),
            _const_spec((1, d_model)),
            _const_spec(w1.shape),
            _const_spec(w2.shape),
        ],
        out_specs=pl.BlockSpec((1, tm, d_model), lambda i, j: (i, j, 0)),
        out_shape=jax.ShapeDtypeStruct((b, s, d_model), F32),
        compiler_params=pltpu.CompilerParams(
            dimension_semantics=("arbitrary", "arbitrary"),
            vmem_limit_bytes=VMEM_LIMIT_BYTES),
        name="out_mlp",
    )(x, ya, yb, w_out.astype(BF16), norm_g.reshape(1, d_model), w1.astype(BF16), w2.astype(BF16))


def kernel(x, mem, norm_mix, norm_mem, w_mem_kv, xa_q_gain, xa_k_gain, w_out, norm_mlp, w_mlp_in, w_mlp_out, w_in_a, conv_w, conv_b, w_rgate, b_rgate, w_igate, b_igate, lru_lambda, w_in_b, b_forget, fox_q_gain, fox_k_gain):
    depth = norm_mix.shape[0]
    s = x.shape[1]
    ts = min(s, 512)
    t_attn = min(s, 512)
    k_mem, v_mem = _mem_kv(mem, norm_mem, w_mem_kv, xa_k_gain)
    for layer in range(depth):
        j = layer // 2
        if layer % 2 == 0:
            ya, yb = _lru_front(x, norm_mix[layer], w_in_a[j], conv_w[j], conv_b[j], w_rgate[j], b_rgate[j],
                                w_igate[j], b_igate[j], lru_lambda[j], k_mem[layer], v_mem[layer],
                                xa_q_gain[layer], ts)
        else:
            qt, kaug, vt, yb = _fox_front(x, norm_mix[layer], w_in_b[j], b_forget[j], fox_q_gain[j],
                                          fox_k_gain[j], k_mem[layer], v_mem[layer], xa_q_gain[layer], ts)
            ya = _fox_attn(qt, kaug, vt, fox_q_gain.shape[-1], t_attn)
        x = _out_mlp(x, ya, yb, w_out[layer], norm_mlp[layer], w_mlp_in[layer], w_mlp_out[layer], ts,
                     ya_transposed=layer % 2 == 1)
    return x
```

```python
import functools
import math

import jax
import jax.numpy as jnp
from jax import lax
from jax.experimental import pallas as pl
from jax.experimental.pallas import tpu as pltpu

EPS = 1e-6
NEG_INF = -1e30
LRU_C = 8.0
CONV_WIDTH = 4
LOG2E = math.log2(math.e)
SUBLANES = 8
LANES = 128
BF16_ROWS = 16
MXU_TILE = 256
VMEM_LIMIT_BYTES = 56 * 1024 * 1024

F32 = jnp.float32
BF16 = jnp.bfloat16


def _rms(x):
    return x * lax.rsqrt(jnp.mean(x * x, axis=-1, keepdims=True) + EPS)


def _dot(a, b):
    return jnp.dot(a, b, preferred_element_type=F32)


def _dot_nt(a, b):
    return lax.dot_general(a, b, (((1,), (1,)), ((), ())), preferred_element_type=F32)


def _sigmoid(x):
    return 0.5 * jnp.tanh(0.5 * x) + 0.5


def _softplus(x):
    return jnp.maximum(x, 0.0) + jnp.log(1.0 + jnp.exp(-jnp.abs(x)))


def _gelu_tanh(x):
    c = math.sqrt(2.0 / math.pi)
    return x * (0.5 * (1.0 + jnp.tanh(c * (x + 0.044715 * (x * x * x)))))


def _const_spec(shape):
    nd = len(shape)
    return pl.BlockSpec(shape, lambda *_: (0,) * nd)


def _mem_kv_kernel(mem_ref, g_ref, w_ref, kg_ref, k_ref, v_ref, *, heads, hd):
    width = heads * hd
    hn = (_rms(mem_ref[0]) * g_ref[0]).astype(BF16)
    kv = _dot(hn, w_ref[0])
    for h in range(heads):
        kh = kv[:, h * hd:(h + 1) * hd]
        k_ref[0, 0, :, h * hd:(h + 1) * hd] = (_rms(kh) * kg_ref[0]).astype(BF16)
    v_ref[0, 0] = kv[:, width:].astype(BF16)


def _mem_kv(mem, norm_mem, w_mem_kv, xa_k_gain):
    depth, d_model, two_w = w_mem_kv.shape
    b, m, _ = mem.shape
    width = two_w // 2
    hd = xa_k_gain.shape[-1]
    heads = width // hd
    out = jax.ShapeDtypeStruct((depth, b, m, width), BF16)
    return pl.pallas_call(
        functools.partial(_mem_kv_kernel, heads=heads, hd=hd),
        grid=(depth, b),
        in_specs=[
            pl.BlockSpec((1, m, d_model), lambda l, i: (i, 0, 0)),
            pl.BlockSpec((1, 1, d_model), lambda l, i: (l, 0, 0)),
            pl.BlockSpec((1, d_model, two_w), lambda l, i: (l, 0, 0)),
            pl.BlockSpec((1, 1, hd), lambda l, i: (l, 0, 0)),
        ],
        out_specs=[
            pl.BlockSpec((1, 1, m, width), lambda l, i: (l, i, 0, 0)),
            pl.BlockSpec((1, 1, m, width), lambda l, i: (l, i, 0, 0)),
        ],
        out_shape=[out, out],
        compiler_params=pltpu.CompilerParams(
            dimension_semantics=("arbitrary", "arbitrary"),
            vmem_limit_bytes=VMEM_LIMIT_BYTES),
        name="mem_kv",
    )(mem, norm_mem.reshape(depth, 1, d_model), w_mem_kv.astype(BF16),
      xa_k_gain.reshape(depth, 1, hd))


def _mem_attn(xq, k_ref, v_ref, qg, heads, hd):
    scale = LOG2E / math.sqrt(hd)
    outs = []
    for h in range(heads):
        q = (_rms(xq[:, h * hd:(h + 1) * hd]) * (qg * scale)).astype(BF16)
        s = _dot_nt(q, k_ref[0, :, h * hd:(h + 1) * hd])
        p = jnp.exp2(s - jnp.max(s, axis=-1, keepdims=True))
        l = jnp.sum(p, axis=-1, keepdims=True)
        o = _dot(p.astype(BF16), v_ref[0, :, h * hd:(h + 1) * hd])
        outs.append(o / l)
    return jnp.concatenate(outs, axis=1)


def _lru_front_kernel(x_ref, g_ref, w_in_ref, cw_ref, cb_ref, wg_ref, br_ref, bi_ref,
                      lam_ref, k_ref, v_ref, qg_ref, y_ref, ym_ref,
                      u_tail, a_s, b_s, hc_ref, *, ts, pitch, d_lru, xa_heads, xa_hd):
    half = d_lru // 2

    @pl.when(pl.program_id(1) == 0)
    def _():
        u_tail[...] = jnp.zeros_like(u_tail)
        hc_ref[...] = jnp.zeros_like(hc_ref)

    h = (_rms(x_ref[0]) * g_ref[...]).astype(BF16)
    proj = _dot(h, w_in_ref[...])
    u = proj[:, :d_lru]
    gate = proj[:, d_lru:2 * d_lru]
    xq = proj[:, 2 * d_lru:]

    tail = u_tail[...]
    row8 = lax.broadcasted_iota(jnp.int32, (SUBLANES, d_lru), 0)
    uc = cb_ref[...] + cw_ref[CONV_WIDTH - 1:CONV_WIDTH, :] * u
    for d in range(1, CONV_WIDTH):
        rolled = pltpu.roll(u, d, 0)
        top = jnp.where(row8 < d, pltpu.roll(tail, d, 0), rolled[:SUBLANES])
        shifted = jnp.concatenate([top, rolled[SUBLANES:]], axis=0)
        uc = uc + cw_ref[CONV_WIDTH - 1 - d:CONV_WIDTH - d, :] * shifted
    u_tail[...] = u[ts - SUBLANES:, :]

    ub = uc.astype(BF16)
    g0 = _dot(ub[:, :half], wg_ref[0])
    g1 = _dot(ub[:, half:], wg_ref[1])
    r_pre = jnp.concatenate([g0[:, :half], g1[:, :half]], axis=1)
    i_pre = jnp.concatenate([g0[:, half:], g1[:, half:]], axis=1)
    r = _sigmoid(r_pre + br_ref[...])
    i = _sigmoid(i_pre + bi_ref[...])
    log_a = (-LRU_C) * r * _softplus(-lam_ref[...])
    a = jnp.exp(log_a)
    b = jnp.exp2((0.5 * LOG2E) * jnp.log(1.0 - a * a)) * (i * uc)

    seg = ts // SUBLANES
    nslab = d_lru // LANES
    for c in range(nslab):
        for sg in range(SUBLANES):
            a_s[c, pitch * sg:pitch * sg + seg, :] = a[seg * sg:seg * (sg + 1), LANES * c:LANES * (c + 1)]
            b_s[c, pitch * sg:pitch * sg + seg, :] = b[seg * sg:seg * (sg + 1), LANES * c:LANES * (c + 1)]

    def step(k, carry):
        out = []
        for c in range(nslab):
            rows = pl.ds(k, SUBLANES, stride=pitch)
            hv, pv = carry[c]
            av = a_s[c, rows, :]
            hv = av * hv + b_s[c, rows, :]
            pv = av * pv
            a_s[c, rows, :] = pv
            b_s[c, rows, :] = hv
            out.append((hv, pv))
        return tuple(out)

    zero = jnp.zeros((SUBLANES, LANES), F32)
    ends = lax.fori_loop(0, seg, step, ((zero, zero + 1.0),) * nslab, unroll=True)

    row8 = lax.broadcasted_iota(jnp.int32, (SUBLANES, LANES), 0)
    gl = _gelu_tanh(gate)
    for c in range(nslab):
        bv, av = ends[c]
        for d in (1, 2, 4):
            keep = row8 >= d
            a_sh = jnp.where(keep, pltpu.roll(av, d, 0), 1.0)
            b_sh = jnp.where(keep, pltpu.roll(bv, d, 0), 0.0)
            bv = av * b_sh + bv
            av = av * a_sh
        hc = hc_ref[:, LANES * c:LANES * (c + 1)]
        end_state = bv + av * hc
        enter = jnp.where(row8 == 0, hc, pltpu.roll(end_state, 1, 0))
        hc_ref[:, LANES * c:LANES * (c + 1)] = end_state[SUBLANES - 1:SUBLANES, :]
        for sg in range(SUBLANES):
            rows = slice(pitch * sg, pitch * sg + seg)
            hs = b_s[c, rows, :] + a_s[c, rows, :] * enter[sg:sg + 1, :]
            y_ref[0, seg * sg:seg * (sg + 1), LANES * c:LANES * (c + 1)] = (
                hs * gl[seg * sg:seg * (sg + 1), LANES * c:LANES * (c + 1)]).astype(BF16)
    ym_ref[0] = _mem_attn(xq, k_ref, v_ref, qg_ref[...], xa_heads, xa_hd).astype(BF16)


def _lru_front(x, norm_g, w_in, conv_w, conv_b, w_r, b_r, w_i, b_i, lam, k_mem, v_mem, xa_qg, ts):
    b, s, d_model = x.shape
    d_lru = conv_w.shape[-1]
    half = d_lru // 2
    xa_hd = xa_qg.shape[-1]
    xa_width = w_in.shape[1] - 2 * d_lru
    xa_heads = xa_width // xa_hd
    m = k_mem.shape[1]
    blk = w_r.shape[-1]
    nblk_half = half // blk
    wg = jnp.stack([
        jnp.concatenate([jax.scipy.linalg.block_diag(*w_r[hf * nblk_half:(hf + 1) * nblk_half]),
                         jax.scipy.linalg.block_diag(*w_i[hf * nblk_half:(hf + 1) * nblk_half])], axis=1)
        for hf in range(2)]).astype(BF16)
    seg = ts // SUBLANES
    pitch = seg + SUBLANES if (seg // SUBLANES) % 2 == 0 else seg + 2 * SUBLANES
    row = lambda v: v.reshape(1, -1)
    out = jax.ShapeDtypeStruct((b, s, d_lru), BF16)
    outm = jax.ShapeDtypeStruct((b, s, xa_width), BF16)
    return pl.pallas_call(
        functools.partial(_lru_front_kernel, ts=ts, pitch=pitch, d_lru=d_lru, xa_heads=xa_heads,
                          xa_hd=xa_hd),
        grid=(b, s // ts),
        in_specs=[
            pl.BlockSpec((1, ts, d_model), lambda i, j: (i, j, 0)),
            _const_spec((1, d_model)),
            _const_spec(w_in.shape),
            _const_spec(conv_w.shape),
            _const_spec((1, d_lru)),
            _const_spec(wg.shape),
            _const_spec((1, d_lru)),
            _const_spec((1, d_lru)),
            _const_spec((1, d_lru)),
            pl.BlockSpec((1, m, xa_width), lambda i, j: (i, 0, 0)),
            pl.BlockSpec((1, m, xa_width), lambda i, j: (i, 0, 0)),
            _const_spec((1, xa_hd)),
        ],
        out_specs=[
            pl.BlockSpec((1, ts, d_lru), lambda i, j: (i, j, 0)),
            pl.BlockSpec((1, ts, xa_width), lambda i, j: (i, j, 0)),
        ],
        out_shape=[out, outm],
        scratch_shapes=[
            pltpu.VMEM((SUBLANES, d_lru), F32),
            pltpu.VMEM((d_lru // LANES, SUBLANES * pitch, LANES), F32),
            pltpu.VMEM((d_lru // LANES, SUBLANES * pitch, LANES), F32),
            pltpu.VMEM((1, d_lru), F32),
        ],
        compiler_params=pltpu.CompilerParams(
            dimension_semantics=("arbitrary", "arbitrary"),
            vmem_limit_bytes=VMEM_LIMIT_BYTES),
        name="lru_front",
    )(x, row(norm_g), w_in.astype(BF16), conv_w, row(conv_b), wg, row(b_r), row(b_i), row(lam),
      k_mem, v_mem, row(xa_qg))


def _fox_front_kernel(x_ref, g_ref, w_ref, hm_ref, qg_ref, kg_ref, bf_ref, pc_ref, k_ref, v_ref,
                      xqg_ref, qt_out, ka_out, vt_out, ym_ref, c_carry,
                      *, ts, width, xa_heads, xa_hd):
    @pl.when(pl.program_id(1) == 0)
    def _():
        c_carry[...] = jnp.zeros_like(c_carry)

    h = (_rms(x_ref[0]) * g_ref[...]).astype(BF16)
    proj = _dot(h, w_ref[...])
    q = proj[:, :width]
    k = proj[:, width:2 * width]
    v = proj[:, 2 * width:3 * width]
    xq = proj[:, 3 * width:3 * width + xa_heads * xa_hd]
    f = proj[:, 3 * width + xa_heads * xa_hd:]

    def head_ms(t):
        t2 = (t * t).astype(BF16)
        return jnp.concatenate(
            [_dot(t2[:, c:c + MXU_TILE], hm_ref[...]) for c in range(0, width, MXU_TILE)], axis=1)

    qt_out[0] = jnp.transpose(q * lax.rsqrt(head_ms(q) + EPS) * qg_ref[...]).astype(BF16)
    vt_out[0] = jnp.transpose(v).astype(BF16)
    kn = (k * lax.rsqrt(head_ms(k) + EPS) * kg_ref[...]).astype(BF16)

    c = -_softplus(-(f + bf_ref[...]))
    row = lax.broadcasted_iota(jnp.int32, c.shape, 0)
    d = 1
    while d < ts:
        c = c + jnp.where(row >= d, pltpu.roll(c, d, 0), 0.0)
        d *= 2
    c = c + c_carry[...]
    c_carry[...] = c[ts - 1:ts, :]

    b2 = c * (-LOG2E)
    hi = b2.astype(BF16).astype(F32)
    r1 = b2 - hi
    mid = r1.astype(BF16).astype(F32)
    piece = pc_ref[...]
    pieces = jnp.where(piece == 0, hi, jnp.where(piece == 1, mid, r1 - mid))
    for p in range(width // LANES):
        bias = pieces if p == 0 else pltpu.roll(pieces, LANES - 6 * p, 1)
        ka_out[0, :, 2 * LANES * p:2 * LANES * p + LANES] = kn[:, LANES * p:LANES * (p + 1)]
        ka_out[0, :, 2 * LANES * p + LANES:2 * LANES * (p + 1)] = bias.astype(BF16)

    ym_ref[0] = _mem_attn(xq, k_ref, v_ref, xqg_ref[...], xa_heads, xa_hd).astype(BF16)


def _fox_front(x, norm_g, w_in, b_f, q_gain, k_gain, k_mem, v_mem, xa_qg, ts):
    b, s, d_model = x.shape
    hd = q_gain.shape[-1]
    heads = b_f.shape[-1]
    width = heads * hd
    pairs = width // LANES
    xa_hd = xa_qg.shape[-1]
    xa_width = w_in.shape[1] - 3 * width - heads
    xa_heads = xa_width // xa_hd
    m = k_mem.shape[1]
    w_f = jnp.repeat(w_in[:, 3 * width:3 * width + heads], 3, axis=1)
    w_all = jnp.concatenate(
        [w_in[:, :3 * width], w_in[:, 3 * width + heads:], w_f,
         jnp.zeros((d_model, LANES - 3 * heads), w_in.dtype)], axis=1).astype(BF16)
    hid = jnp.arange(MXU_TILE) // hd
    head_mean = jnp.where(hid[:, None] == hid[None, :], 1.0 / hd, 0.0).astype(BF16)
    piece = (jnp.arange(LANES, dtype=jnp.int32) % 3).reshape(1, LANES)
    qg = jnp.tile(q_gain * (LOG2E / math.sqrt(hd)), heads).reshape(1, width)
    kg = jnp.tile(k_gain, heads).reshape(1, width)
    bfp = jnp.zeros((1, LANES), F32).at[0, :3 * heads].set(jnp.repeat(b_f, 3))
    row = lambda t: t.reshape(1, -1)
    act_t = jax.ShapeDtypeStruct((b, width, s), BF16)
    return pl.pallas_call(
        functools.partial(_fox_front_kernel, ts=ts, width=width, xa_heads=xa_heads, xa_hd=xa_hd),
        grid=(b, s // ts),
        in_specs=[
            pl.BlockSpec((1, ts, d_model), lambda i, j: (i, j, 0)),
            _const_spec((1, d_model)),
            _const_spec(w_all.shape),
            _const_spec(head_mean.shape),
            _const_spec((1, width)),
            _const_spec((1, width)),
            _const_spec((1, LANES)),
            _const_spec((1, LANES)),
            pl.BlockSpec((1, m, xa_width), lambda i, j: (i, 0, 0)),
            pl.BlockSpec((1, m, xa_width), lambda i, j: (i, 0, 0)),
            _const_spec((1, xa_hd)),
        ],
        out_specs=[
            pl.BlockSpec((1, width, ts), lambda i, j: (i, 0, j)),
            pl.BlockSpec((1, ts, 2 * width), lambda i, j: (i, j, 0)),
            pl.BlockSpec((1, width, ts), lambda i, j: (i, 0, j)),
            pl.BlockSpec((1, ts, xa_width), lambda i, j: (i, j, 0)),
        ],
        out_shape=[act_t, jax.ShapeDtypeStruct((b, s, 2 * width), BF16), act_t,
                   jax.ShapeDtypeStruct((b, s, xa_width), BF16)],
        scratch_shapes=[pltpu.VMEM((1, LANES), F32)],
        compiler_params=pltpu.CompilerParams(
            dimension_semantics=("arbitrary", "arbitrary"),
            vmem_limit_bytes=VMEM_LIMIT_BYTES),
        name="fox_front",
    )(x, row(norm_g), w_all, head_mean, qg, kg, bfp, piece, k_mem, v_mem, row(xa_qg))


def _fox_attn_kernel(qt_ref, ka_ref, vt_ref, o_ref, s0_ref, s1_ref, m_ref, acc_ref, *, t, hd, heads):
    i = pl.program_id(2)
    rowq = lax.broadcasted_iota(jnp.int32, (2 * hd, t), 0)
    keys = lax.broadcasted_iota(jnp.int32, (t, t), 0)
    queries = lax.broadcasted_iota(jnp.int32, (t, t), 1)

    def weights(h):
        slot = h % 2
        qt = qt_ref[0, 2 * hd * (h // 2):2 * hd * (h // 2 + 1), :]
        top = jnp.where((rowq >= hd * slot) & (rowq < hd * (slot + 1)), qt, jnp.zeros_like(qt))
        bot = jnp.where((rowq >= 3 * slot) & (rowq < 3 * slot + 3), 1.0, 0.0).astype(BF16)
        return jnp.concatenate([top, bot], axis=0)

    w = [weights(h) for h in range(heads)]

    def logits(j, s_ref):
        rows = pl.ds(pl.multiple_of(j * t, t), t)
        for h in range(heads):
            ka = ka_ref[0, rows, 4 * hd * (h // 2):4 * hd * (h // 2 + 1)]
            s_ref[h] = _dot(ka, w[h])

    def softmax_pv(j, s_ref, masked):
        cols = pl.ds(pl.multiple_of(j * t, t), t)
        for h in range(heads):
            st = s_ref[h]
            if masked:
                st = jnp.where(keys <= queries, st, NEG_INF)
            m = m_ref[h]
            mn = jnp.maximum(m, jnp.max(st, axis=0, keepdims=True))
            pt = jnp.exp2(st - mn)
            al = jnp.exp2(m - mn)
            m_ref[h] = mn
            vt = jnp.concatenate([vt_ref[0, hd * h:hd * (h + 1), cols], ones_row], axis=0)
            acc_ref[h] = al * acc_ref[h] + _dot(vt, pt.astype(BF16))

    m_ref[...] = jnp.full(m_ref.shape, NEG_INF, F32)
    acc_ref[...] = jnp.zeros_like(acc_ref)
    ones_row = (lax.broadcasted_iota(jnp.int32, (BF16_ROWS, t), 0) == 0).astype(BF16)

    logits(0, s0_ref)

    def two_blocks(jj, carry):
        j = 2 * jj
        logits(j + 1, s1_ref)
        softmax_pv(j, s0_ref, False)
        logits(j + 2, s0_ref)
        softmax_pv(j + 1, s1_ref, False)
        return carry

    lax.fori_loop(0, lax.shift_right_logical(i, 1), two_blocks, 0)

    @pl.when(lax.bitwise_and(i, 1) == 0)
    def _():
        softmax_pv(i, s0_ref, True)

    @pl.when(lax.bitwise_and(i, 1) == 1)
    def _():
        logits(i, s1_ref)
        softmax_pv(i - 1, s0_ref, False)
        softmax_pv(i, s1_ref, True)

    for h in range(heads):
        o_ref[0, hd * h:hd * (h + 1), :] = (acc_ref[h, :hd, :] / acc_ref[h, hd:hd + 1, :]).astype(BF16)


def _fox_attn(qt, kaug, vt, hd, t, heads_per_step):
    b, width, s = qt.shape
    hps = heads_per_step
    return pl.pallas_call(
        functools.partial(_fox_attn_kernel, t=t, hd=hd, heads=hps),
        grid=(b, width // (hps * hd), s // t),
        in_specs=[
            pl.BlockSpec((1, hps * hd, t), lambda bi, p, i: (bi, p, i)),
            pl.BlockSpec((1, s, 2 * hps * hd), lambda bi, p, i: (bi, 0, p)),
            pl.BlockSpec((1, hps * hd, s), lambda bi, p, i: (bi, p, 0)),
        ],
        out_specs=pl.BlockSpec((1, hps * hd, t), lambda bi, p, i: (bi, p, i)),
        out_shape=jax.ShapeDtypeStruct((b, width, s), BF16),
        scratch_shapes=[
            pltpu.VMEM((hps, t, t), F32),
            pltpu.VMEM((hps, t, t), F32),
            pltpu.VMEM((hps, 1, t), F32),
            pltpu.VMEM((hps, hd + BF16_ROWS, t), F32),
        ],
        compiler_params=pltpu.CompilerParams(
            dimension_semantics=("arbitrary", "arbitrary", "arbitrary"),
            vmem_limit_bytes=VMEM_LIMIT_BYTES),
        name="fox_attn",
    )(qt, kaug, vt)


def _out_mlp_kernel(x_ref, ya_ref, yb_ref, wo_ref, g_ref, w1_ref, w2_ref, o_ref,
                    *, wa, ff_chunk, ya_transposed):
    if ya_transposed:
        mix = lax.dot_general(ya_ref[0], wo_ref[:wa, :], (((0,), (0,)), ((), ())),
                              preferred_element_type=F32)
    else:
        mix = _dot(ya_ref[0], wo_ref[:wa, :])
    x1 = x_ref[0] + mix + _dot(yb_ref[0], wo_ref[wa:, :])
    h = (_rms(x1) * g_ref[...]).astype(BF16)
    d_ff = w1_ref.shape[1]
    o_ref[0] = x1
    for c0 in range(0, d_ff, ff_chunk):
        a = jnp.maximum(_dot(h, w1_ref[:, c0:c0 + ff_chunk]), 0.0)
        o_ref[0] += _dot((a * a).astype(BF16), w2_ref[c0:c0 + ff_chunk, :])


def _out_mlp(x, ya, yb, w_out, norm_g, w1, w2, tm, ya_transposed):
    b, s, d_model = x.shape
    wb = yb.shape[-1]
    wa = w_out.shape[0] - wb
    d_ff = w1.shape[1]
    if ya_transposed:
        ya_spec = pl.BlockSpec((1, wa, tm), lambda i, j: (i, 0, j))
    else:
        ya_spec = pl.BlockSpec((1, tm, wa), lambda i, j: (i, j, 0))
    return pl.pallas_call(
        functools.partial(_out_mlp_kernel, wa=wa, ff_chunk=min(d_ff, 1024), ya_transposed=ya_transposed),
        grid=(b, s // tm),
        in_specs=[
            pl.BlockSpec((1, tm, d_model), lambda i, j: (i, j, 0)),
            ya_spec,
            pl.BlockSpec((1, tm, wb), lambda i, j: (i, j, 0)),
            _const_spec(w_out.shape),
            _const_spec((1, d_model)),
            _const_spec(w1.shape),
            _const_spec(w2.shape),
        ],
        out_specs=pl.BlockSpec((1, tm, d_model), lambda i, j: (i, j, 0)),
        out_shape=jax.ShapeDtypeStruct((b, s, d_model), F32),
        compiler_params=pltpu.CompilerParams(
            dimension_semantics=("arbitrary", "arbitrary"),
            vmem_limit_bytes=VMEM_LIMIT_BYTES),
        name="out_mlp",
    )(x, ya, yb, w_out.astype(BF16), norm_g.reshape(1, d_model), w1.astype(BF16), w2.astype(BF16))


def kernel(x, mem, norm_mix, norm_mem, w_mem_kv, xa_q_gain, xa_k_gain, w_out, norm_mlp, w_mlp_in, w_mlp_out, w_in_a, conv_w, conv_b, w_rgate, b_rgate, w_igate, b_igate, lru_lambda, w_in_b, b_forget, fox_q_gain, fox_k_gain):
    depth = norm_mix.shape[0]
    s = x.shape[1]
    ts = min(s, 512)
    t_attn = min(s, 512)
    k_mem, v_mem = _mem_kv(mem, norm_mem, w_mem_kv, xa_k_gain)
    for layer in range(depth):
        j = layer // 2
        if layer % 2 == 0:
            ya, yb = _lru_front(x, norm_mix[layer], w_in_a[j], conv_w[j], conv_b[j], w_rgate[j], b_rgate[j],
                                w_igate[j], b_igate[j], lru_lambda[j], k_mem[layer], v_mem[layer],
                                xa_q_gain[layer], ts)
        else:
            qt, kaug, vt, yb = _fox_front(x, norm_mix[layer], w_in_b[j], b_forget[j], fox_q_gain[j],
                                          fox_k_gain[j], k_mem[layer], v_mem[layer], xa_q_gain[layer], ts)
            ya = _fox_attn(qt, kaug, vt, fox_q_gain.shape[-1], t_attn, heads_per_step=4)
        x = _out_mlp(x, ya, yb, w_out[layer], norm_mlp[layer], w_mlp_in[layer], w_mlp_out[layer], ts,
                     ya_transposed=layer % 2 == 1)
    return x
```

```python
import functools
import math

import jax
import jax.numpy as jnp
from jax import lax
from jax.experimental import pallas as pl
from jax.experimental.pallas import tpu as pltpu

EPS = 1e-6
NEG_INF = -1e30
LRU_C = 8.0
CONV_WIDTH = 4
LOG2E = math.log2(math.e)
SUBLANES = 8
LANES = 128
BF16_ROWS = 16
MXU_TILE = 256
VMEM_LIMIT_BYTES = 56 * 1024 * 1024

F32 = jnp.float32
BF16 = jnp.bfloat16


def _rms(x):
    return x * lax.rsqrt(jnp.mean(x * x, axis=-1, keepdims=True) + EPS)


def _dot(a, b):
    return jnp.dot(a, b, preferred_element_type=F32)


def _dot_nt(a, b):
    return lax.dot_general(a, b, (((1,), (1,)), ((), ())), preferred_element_type=F32)


def _sigmoid(x):
    return 0.5 * jnp.tanh(0.5 * x) + 0.5


def _softplus(x):
    return jnp.maximum(x, 0.0) + jnp.log(1.0 + jnp.exp(-jnp.abs(x)))


def _gelu_tanh(x):
    c = math.sqrt(2.0 / math.pi)
    return x * (0.5 * (1.0 + jnp.tanh(c * (x + 0.044715 * (x * x * x)))))


def _const_spec(shape):
    nd = len(shape)
    return pl.BlockSpec(shape, lambda *_: (0,) * nd)


def _mem_kv_kernel(mem_ref, g_ref, w_ref, kg_ref, k_ref, v_ref, *, heads, hd):
    width = heads * hd
    hn = (_rms(mem_ref[0]) * g_ref[0]).astype(BF16)
    kv = _dot(hn, w_ref[0])
    for h in range(heads):
        kh = kv[:, h * hd:(h + 1) * hd]
        k_ref[0, 0, :, h * hd:(h + 1) * hd] = (_rms(kh) * kg_ref[0]).astype(BF16)
    v_ref[0, 0] = kv[:, width:].astype(BF16)


def _mem_kv(mem, norm_mem, w_mem_kv, xa_k_gain):
    depth, d_model, two_w = w_mem_kv.shape
    b, m, _ = mem.shape
    width = two_w // 2
    hd = xa_k_gain.shape[-1]
    heads = width // hd
    out = jax.ShapeDtypeStruct((depth, b, m, width), BF16)
    return pl.pallas_call(
        functools.partial(_mem_kv_kernel, heads=heads, hd=hd),
        grid=(depth, b),
        in_specs=[
            pl.BlockSpec((1, m, d_model), lambda l, i: (i, 0, 0)),
            pl.BlockSpec((1, 1, d_model), lambda l, i: (l, 0, 0)),
            pl.BlockSpec((1, d_model, two_w), lambda l, i: (l, 0, 0)),
            pl.BlockSpec((1, 1, hd), lambda l, i: (l, 0, 0)),
        ],
        out_specs=[
            pl.BlockSpec((1, 1, m, width), lambda l, i: (l, i, 0, 0)),
            pl.BlockSpec((1, 1, m, width), lambda l, i: (l, i, 0, 0)),
        ],
        out_shape=[out, out],
        compiler_params=pltpu.CompilerParams(
            dimension_semantics=("arbitrary", "arbitrary"),
            vmem_limit_bytes=VMEM_LIMIT_BYTES),
        name="mem_kv",
    )(mem, norm_mem.reshape(depth, 1, d_model), w_mem_kv.astype(BF16),
      xa_k_gain.reshape(depth, 1, hd))


def _mem_attn(xq, k_ref, v_ref, qg, heads, hd):
    scale = LOG2E / math.sqrt(hd)
    outs = []
    for h in range(heads):
        q = (_rms(xq[:, h * hd:(h + 1) * hd]) * (qg * scale)).astype(BF16)
        s = _dot_nt(q, k_ref[0, :, h * hd:(h + 1) * hd])
        p = jnp.exp2(s - jnp.max(s, axis=-1, keepdims=True))
        l = jnp.sum(p, axis=-1, keepdims=True)
        o = _dot(p.astype(BF16), v_ref[0, :, h * hd:(h + 1) * hd])
        outs.append(o / l)
    return jnp.concatenate(outs, axis=1)


def _lru_front_kernel(x_ref, g_ref, w_in_ref, cw_ref, cb_ref, wg_ref, br_ref, bi_ref,
                      lam_ref, k_ref, v_ref, qg_ref, y_ref, ym_ref,
                      u_tail, a_s, b_s, hc_ref, *, ts, pitch, d_lru, xa_heads, xa_hd):
    half = d_lru // 2

    @pl.when(pl.program_id(1) == 0)
    def _():
        u_tail[...] = jnp.zeros_like(u_tail)
        hc_ref[...] = jnp.zeros_like(hc_ref)

    h = (_rms(x_ref[0]) * g_ref[...]).astype(BF16)
    proj = _dot(h, w_in_ref[...])
    u = proj[:, :d_lru]
    gate = proj[:, d_lru:2 * d_lru]
    xq = proj[:, 2 * d_lru:]

    tail = u_tail[...]
    row8 = lax.broadcasted_iota(jnp.int32, (SUBLANES, d_lru), 0)
    uc = cb_ref[...] + cw_ref[CONV_WIDTH - 1:CONV_WIDTH, :] * u
    for d in range(1, CONV_WIDTH):
        rolled = pltpu.roll(u, d, 0)
        top = jnp.where(row8 < d, pltpu.roll(tail, d, 0), rolled[:SUBLANES])
        shifted = jnp.concatenate([top, rolled[SUBLANES:]], axis=0)
        uc = uc + cw_ref[CONV_WIDTH - 1 - d:CONV_WIDTH - d, :] * shifted
    u_tail[...] = u[ts - SUBLANES:, :]

    ub = uc.astype(BF16)
    g0 = _dot(ub[:, :half], wg_ref[0])
    g1 = _dot(ub[:, half:], wg_ref[1])
    r_pre = jnp.concatenate([g0[:, :half], g1[:, :half]], axis=1)
    i_pre = jnp.concatenate([g0[:, half:], g1[:, half:]], axis=1)
    r = _sigmoid(r_pre + br_ref[...])
    i = _sigmoid(i_pre + bi_ref[...])
    log_a = (-LRU_C) * r * _softplus(-lam_ref[...])
    a = jnp.exp(log_a)
    b = jnp.exp2((0.5 * LOG2E) * jnp.log(1.0 - a * a)) * (i * uc)

    seg = ts // SUBLANES
    nslab = d_lru // LANES
    for c in range(nslab):
        for sg in range(SUBLANES):
            a_s[c, pitch * sg:pitch * sg + seg, :] = a[seg * sg:seg * (sg + 1), LANES * c:LANES * (c + 1)]
            b_s[c, pitch * sg:pitch * sg + seg, :] = b[seg * sg:seg * (sg + 1), LANES * c:LANES * (c + 1)]

    def step(k, carry):
        out = []
        for c in range(nslab):
            rows = pl.ds(k, SUBLANES, stride=pitch)
            hv, pv = carry[c]
            av = a_s[c, rows, :]
            hv = av * hv + b_s[c, rows, :]
            pv = av * pv
            a_s[c, rows, :] = pv
            b_s[c, rows, :] = hv
            out.append((hv, pv))
        return tuple(out)

    zero = jnp.zeros((SUBLANES, LANES), F32)
    ends = lax.fori_loop(0, seg, step, ((zero, zero + 1.0),) * nslab, unroll=True)

    row8 = lax.broadcasted_iota(jnp.int32, (SUBLANES, LANES), 0)
    gl = _gelu_tanh(gate)
    for c in range(nslab):
        bv, av = ends[c]
        for d in (1, 2, 4):
            keep = row8 >= d
            a_sh = jnp.where(keep, pltpu.roll(av, d, 0), 1.0)
            b_sh = jnp.where(keep, pltpu.roll(bv, d, 0), 0.0)
            bv = av * b_sh + bv
            av = av * a_sh
        hc = hc_ref[:, LANES * c:LANES * (c + 1)]
        end_state = bv + av * hc
        enter = jnp.where(row8 == 0, hc, pltpu.roll(end_state, 1, 0))
        hc_ref[:, LANES * c:LANES * (c + 1)] = end_state[SUBLANES - 1:SUBLANES, :]
        for sg in range(SUBLANES):
            rows = slice(pitch * sg, pitch * sg + seg)
            hs = b_s[c, rows, :] + a_s[c, rows, :] * enter[sg:sg + 1, :]
            y_ref[0, seg * sg:seg * (sg + 1), LANES * c:LANES * (c + 1)] = (
                hs * gl[seg * sg:seg * (sg + 1), LANES * c:LANES * (c + 1)]).astype(BF16)
    ym_ref[0] = _mem_attn(xq, k_ref, v_ref, qg_ref[...], xa_heads, xa_hd).astype(BF16)


def _lru_front(x, norm_g, w_in, conv_w, conv_b, w_r, b_r, w_i, b_i, lam, k_mem, v_mem, xa_qg, ts):
    b, s, d_model = x.shape
    d_lru = conv_w.shape[-1]
    half = d_lru // 2
    xa_hd = xa_qg.shape[-1]
    xa_width = w_in.shape[1] - 2 * d_lru
    xa_heads = xa_width // xa_hd
    m = k_mem.shape[1]
    blk = w_r.shape[-1]
    nblk_half = half // blk
    wg = jnp.stack([
        jnp.concatenate([jax.scipy.linalg.block_diag(*w_r[hf * nblk_half:(hf + 1) * nblk_half]),
                         jax.scipy.linalg.block_diag(*w_i[hf * nblk_half:(hf + 1) * nblk_half])], axis=1)
        for hf in range(2)]).astype(BF16)
    seg = ts // SUBLANES
    pitch = seg + SUBLANES if (seg // SUBLANES) % 2 == 0 else seg + 2 * SUBLANES
    row = lambda v: v.reshape(1, -1)
    out = jax.ShapeDtypeStruct((b, s, d_lru), BF16)
    outm = jax.ShapeDtypeStruct((b, s, xa_width), BF16)
    return pl.pallas_call(
        functools.partial(_lru_front_kernel, ts=ts, pitch=pitch, d_lru=d_lru, xa_heads=xa_heads,
                          xa_hd=xa_hd),
        grid=(b, s // ts),
        in_specs=[
            pl.BlockSpec((1, ts, d_model), lambda i, j: (i, j, 0)),
            _const_spec((1, d_model)),
            _const_spec(w_in.shape),
            _const_spec(conv_w.shape),
            _const_spec((1, d_lru)),
            _const_spec(wg.shape),
            _const_spec((1, d_lru)),
            _const_spec((1, d_lru)),
            _const_spec((1, d_lru)),
            pl.BlockSpec((1, m, xa_width), lambda i, j: (i, 0, 0)),
            pl.BlockSpec((1, m, xa_width), lambda i, j: (i, 0, 0)),
            _const_spec((1, xa_hd)),
        ],
        out_specs=[
            pl.BlockSpec((1, ts, d_lru), lambda i, j: (i, j, 0)),
            pl.BlockSpec((1, ts, xa_width), lambda i, j: (i, j, 0)),
        ],
        out_shape=[out, outm],
        scratch_shapes=[
            pltpu.VMEM((SUBLANES, d_lru), F32),
            pltpu.VMEM((d_lru // LANES, SUBLANES * pitch, LANES), F32),
            pltpu.VMEM((d_lru // LANES, SUBLANES * pitch, LANES), F32),
            pltpu.VMEM((1, d_lru), F32),
        ],
        compiler_params=pltpu.CompilerParams(
            dimension_semantics=("arbitrary", "arbitrary"),
            vmem_limit_bytes=VMEM_LIMIT_BYTES),
        name="lru_front",
    )(x, row(norm_g), w_in.astype(BF16), conv_w, row(conv_b), wg, row(b_r), row(b_i), row(lam),
      k_mem, v_mem, row(xa_qg))


def _fox_front_kernel(x_ref, g_ref, w_ref, hm_ref, qg_ref, kg_ref, bf_ref, pc_ref, k_ref, v_ref,
                      xqg_ref, qt_out, ka_out, vt_out, nb_out, ym_ref, c_carry,
                      *, ts, width, heads, xa_heads, xa_hd):
    @pl.when(pl.program_id(1) == 0)
    def _():
        c_carry[...] = jnp.zeros_like(c_carry)

    h = (_rms(x_ref[0]) * g_ref[...]).astype(BF16)
    proj = _dot(h, w_ref[...])
    q = proj[:, :width]
    k = proj[:, width:2 * width]
    v = proj[:, 2 * width:3 * width]
    xq = proj[:, 3 * width:3 * width + xa_heads * xa_hd]
    f = proj[:, 3 * width + xa_heads * xa_hd:]

    def head_ms(t):
        t2 = (t * t).astype(BF16)
        return jnp.concatenate(
            [_dot(t2[:, c:c + MXU_TILE], hm_ref[...]) for c in range(0, width, MXU_TILE)], axis=1)

    qt_out[0] = jnp.transpose(q * lax.rsqrt(head_ms(q) + EPS) * qg_ref[...]).astype(BF16)
    vt_out[0] = jnp.transpose(v).astype(BF16)
    kn = (k * lax.rsqrt(head_ms(k) + EPS) * kg_ref[...]).astype(BF16)

    c = -_softplus(-(f + bf_ref[...]))
    row = lax.broadcasted_iota(jnp.int32, c.shape, 0)
    d = 1
    while d < ts:
        c = c + jnp.where(row >= d, pltpu.roll(c, d, 0), 0.0)
        d *= 2
    c = c + c_carry[...]
    c_carry[...] = c[ts - 1:ts, :]

    b2 = c * (-LOG2E)
    hi = b2.astype(BF16).astype(F32)
    r1 = b2 - hi
    mid = r1.astype(BF16).astype(F32)
    piece = pc_ref[...]
    pieces = jnp.where(piece == 0, hi, jnp.where(piece == 1, mid, r1 - mid))
    nb_out[0] = -jnp.transpose(pieces)[:SUBLANES * heads, :]
    lane = lax.broadcasted_iota(jnp.int32, (1, LANES), 1)
    one_lanes = (lane >= 4) & (lane < 7)
    for p in range(width // LANES):
        bias = pieces if p == 0 else pltpu.roll(pieces, LANES - 2 * SUBLANES * p, 1)
        ka_out[0, :, 2 * LANES * p:2 * LANES * p + LANES] = kn[:, LANES * p:LANES * (p + 1)]
        ka_out[0, :, 2 * LANES * p + LANES:2 * LANES * (p + 1)] = jnp.where(one_lanes, 1.0, bias).astype(BF16)

    ym_ref[0] = _mem_attn(xq, k_ref, v_ref, xqg_ref[...], xa_heads, xa_hd).astype(BF16)


def _fox_front(x, norm_g, w_in, b_f, q_gain, k_gain, k_mem, v_mem, xa_qg, ts):
    b, s, d_model = x.shape
    hd = q_gain.shape[-1]
    heads = b_f.shape[-1]
    width = heads * hd
    pairs = width // LANES
    xa_hd = xa_qg.shape[-1]
    xa_width = w_in.shape[1] - 3 * width - heads
    xa_heads = xa_width // xa_hd
    m = k_mem.shape[1]
    w_f = jnp.repeat(w_in[:, 3 * width:3 * width + heads], SUBLANES, axis=1)
    w_all = jnp.concatenate(
        [w_in[:, :3 * width], w_in[:, 3 * width + heads:], w_f,
         jnp.zeros((d_model, LANES - SUBLANES * heads), w_in.dtype)], axis=1).astype(BF16)
    hid = jnp.arange(MXU_TILE) // hd
    head_mean = jnp.where(hid[:, None] == hid[None, :], 1.0 / hd, 0.0).astype(BF16)
    piece = (jnp.arange(LANES, dtype=jnp.int32) % SUBLANES).reshape(1, LANES)
    qg = jnp.tile(q_gain * (LOG2E / math.sqrt(hd)), heads).reshape(1, width)
    kg = jnp.tile(k_gain, heads).reshape(1, width)
    bfp = jnp.zeros((1, LANES), F32).at[0, :SUBLANES * heads].set(jnp.repeat(b_f, SUBLANES))
    row = lambda t: t.reshape(1, -1)
    act_t = jax.ShapeDtypeStruct((b, width, s), BF16)
    return pl.pallas_call(
        functools.partial(_fox_front_kernel, ts=ts, width=width, heads=heads, xa_heads=xa_heads,
                          xa_hd=xa_hd),
        grid=(b, s // ts),
        in_specs=[
            pl.BlockSpec((1, ts, d_model), lambda i, j: (i, j, 0)),
            _const_spec((1, d_model)),
            _const_spec(w_all.shape),
            _const_spec(head_mean.shape),
            _const_spec((1, width)),
            _const_spec((1, width)),
            _const_spec((1, LANES)),
            _const_spec((1, LANES)),
            pl.BlockSpec((1, m, xa_width), lambda i, j: (i, 0, 0)),
            pl.BlockSpec((1, m, xa_width), lambda i, j: (i, 0, 0)),
            _const_spec((1, xa_hd)),
        ],
        out_specs=[
            pl.BlockSpec((1, width, ts), lambda i, j: (i, 0, j)),
            pl.BlockSpec((1, ts, 2 * width), lambda i, j: (i, j, 0)),
            pl.BlockSpec((1, width, ts), lambda i, j: (i, 0, j)),
            pl.BlockSpec((1, SUBLANES * heads, ts), lambda i, j: (i, 0, j)),
            pl.BlockSpec((1, ts, xa_width), lambda i, j: (i, j, 0)),
        ],
        out_shape=[act_t, jax.ShapeDtypeStruct((b, s, 2 * width), BF16), act_t,
                   jax.ShapeDtypeStruct((b, SUBLANES * heads, s), F32),
                   jax.ShapeDtypeStruct((b, s, xa_width), BF16)],
        scratch_shapes=[pltpu.VMEM((1, LANES), F32)],
        compiler_params=pltpu.CompilerParams(
            dimension_semantics=("arbitrary", "arbitrary"),
            vmem_limit_bytes=VMEM_LIMIT_BYTES),
        name="fox_front",
    )(x, row(norm_g), w_all, head_mean, qg, kg, bfp, piece, k_mem, v_mem, row(xa_qg))


def _fox_attn_kernel(qt_ref, ka_ref, vt_ref, nb_ref, o_ref, s0_ref, s1_ref, m_ref, acc_ref,
                     *, t, hd, heads, parts):
    i = pl.program_id(2)
    rowq = lax.broadcasted_iota(jnp.int32, (2 * hd, t), 0)
    row8 = lax.broadcasted_iota(jnp.int32, (SUBLANES, t), 0)
    keys = lax.broadcasted_iota(jnp.int32, (t, t), 0)
    queries = lax.broadcasted_iota(jnp.int32, (t, t), 1)
    tp = t // parts

    def weights(h):
        slot = h % 2
        qt = qt_ref[0, 2 * hd * (h // 2):2 * hd * (h // 2 + 1), :]
        top = jnp.where((rowq >= hd * slot) & (rowq < hd * (slot + 1)), qt, jnp.zeros_like(qt))
        nb = pltpu.roll(nb_ref[0, SUBLANES * h:SUBLANES * (h + 1), :], 4, 0)
        ones0 = jnp.where(row8 < 3, 1.0 if slot == 0 else 0.0, 0.0)
        ones1 = jnp.where(row8 < 3, 1.0 if slot == 1 else 0.0, 0.0)
        g0 = jnp.where((row8 >= 4) & (row8 < 7), nb, ones0)
        bot = jnp.concatenate([g0, ones1, jnp.zeros((2 * hd - 2 * SUBLANES, t), F32)], axis=0)
        return jnp.concatenate([top, bot.astype(BF16)], axis=0)

    w = [weights(h) for h in range(heads)]

    def logits(j, s_ref, h, part):
        rows = pl.ds(pl.multiple_of(j * t, t) + tp * part, tp)
        ka = ka_ref[0, rows, 4 * hd * (h // 2):4 * hd * (h // 2 + 1)]
        s_ref[h, tp * part:tp * (part + 1), :] = _dot(ka, w[h]).astype(BF16)

    def head_step(jc, s_cur, masked, h, jn=None, s_next=None):
        st = s_cur[h]
        if masked:
            st = jnp.where(keys <= queries, st, jnp.asarray(NEG_INF, BF16))
        m = m_ref[h]
        mn = jnp.maximum(m, jnp.max(st, axis=0, keepdims=True).astype(F32))
        m_ref[h] = mn
        mb = mn.astype(BF16)
        pv = None
        for part in range(parts):
            if jn is not None:
                logits(jn, s_next, h, part)
            pt = jnp.exp2(st[tp * part:tp * (part + 1), :] - mb)
            cols = pl.ds(pl.multiple_of(jc * t, t) + tp * part, tp)
            vt = jnp.concatenate([vt_ref[0, hd * h:hd * (h + 1), cols], ones_row], axis=0)
            d = _dot(vt, pt)
            pv = d if pv is None else pv + d
        acc_ref[h] = jnp.exp2(m - mn) * acc_ref[h] + pv

    m_ref[...] = jnp.full(m_ref.shape, NEG_INF, F32)
    acc_ref[...] = jnp.zeros_like(acc_ref)
    ones_row = (lax.broadcasted_iota(jnp.int32, (BF16_ROWS, tp), 0) == 0).astype(BF16)

    for h in range(heads):
        for part in range(parts):
            logits(0, s0_ref, h, part)

    def two_blocks(jj, carry):
        j = 2 * jj
        for h in range(heads):
            head_step(j, s0_ref, False, h, j + 1, s1_ref)
        for h in range(heads):
            head_step(j + 1, s1_ref, False, h, j + 2, s0_ref)
        return carry

    lax.fori_loop(0, lax.shift_right_logical(i, 1), two_blocks, 0)

    @pl.when(lax.bitwise_and(i, 1) == 0)
    def _():
        for h in range(heads):
            head_step(i, s0_ref, True, h)

    @pl.when(lax.bitwise_and(i, 1) == 1)
    def _():
        for h in range(heads):
            head_step(i - 1, s0_ref, False, h, i, s1_ref)
        for h in range(heads):
            head_step(i, s1_ref, True, h)


    for h in range(heads):
        o_ref[0, hd * h:hd * (h + 1), :] = (acc_ref[h, :hd, :] / acc_ref[h, hd:hd + 1, :]).astype(BF16)


def _fox_attn(qt, kaug, vt, nb, hd, t, heads_per_step):
    b, width, s = qt.shape
    hps = heads_per_step
    return pl.pallas_call(
        functools.partial(_fox_attn_kernel, t=t, hd=hd, heads=hps, parts=2),
        grid=(b, width // (hps * hd), s // t),
        in_specs=[
            pl.BlockSpec((1, hps * hd, t), lambda bi, p, i: (bi, p, i)),
            pl.BlockSpec((1, s, 2 * hps * hd), lambda bi, p, i: (bi, 0, p)),
            pl.BlockSpec((1, hps * hd, s), lambda bi, p, i: (bi, p, 0)),
            pl.BlockSpec((1, hps * SUBLANES, t), lambda bi, p, i: (bi, p, i)),
        ],
        out_specs=pl.BlockSpec((1, hps * hd, t), lambda bi, p, i: (bi, p, i)),
        out_shape=jax.ShapeDtypeStruct((b, width, s), BF16),
        scratch_shapes=[
            pltpu.VMEM((hps, t, t), BF16),
            pltpu.VMEM((hps, t, t), BF16),
            pltpu.VMEM((hps, 1, t), F32),
            pltpu.VMEM((hps, hd + BF16_ROWS, t), F32),
        ],
        compiler_params=pltpu.CompilerParams(
            dimension_semantics=("arbitrary", "arbitrary", "arbitrary"),
            vmem_limit_bytes=VMEM_LIMIT_BYTES),
        name="fox_attn",
    )(qt, kaug, vt, nb)


def _out_mlp_kernel(x_ref, ya_ref, yb_ref, wo_ref, g_ref, w1_ref, w2_ref, o_ref,
                    *, wa, ff_chunk, ya_transposed):
    if ya_transposed:
        mix = lax.dot_general(ya_ref[0], wo_ref[:wa, :], (((0,), (0,)), ((), ())),
                              preferred_element_type=F32)
    else:
        mix = _dot(ya_ref[0], wo_ref[:wa, :])
    x1 = x_ref[0] + mix + _dot(yb_ref[0], wo_ref[wa:, :])
    h = (_rms(x1) * g_ref[...]).astype(BF16)
    d_ff = w1_ref.shape[1]
    o_ref[0] = x1
    for c0 in range(0, d_ff, ff_chunk):
        a = jnp.maximum(_dot(h, w1_ref[:, c0:c0 + ff_chunk]), 0.0)
        o_ref[0] += _dot((a * a).astype(BF16), w2_ref[c0:c0 + ff_chunk, :])


def _out_mlp(x, ya, yb, w_out, norm_g, w1, w2, tm, ya_transposed):
    b, s, d_model = x.shape
    wb = yb.shape[-1]
    wa = w_out.shape[0] - wb
    d_ff = w1.shape[1]
    if ya_transposed:
        ya_spec = pl.BlockSpec((1, wa, tm), lambda i, j: (i, 0, j))
    else:
        ya_spec = pl.BlockSpec((1, tm, wa), lambda i, j: (i, j, 0))
    return pl.pallas_call(
        functools.partial(_out_mlp_kernel, wa=wa, ff_chunk=min(d_ff, 1024), ya_transposed=ya_transposed),
        grid=(b, s // tm),
        in_specs=[
            pl.BlockSpec((1, tm, d_model), lambda i, j: (i, j, 0)),
            ya_spec,
            pl.BlockSpec((1, tm, wb), lambda i, j: (i, j, 0)),
            _const_spec(w_out.shape),
            _const_spec((1, d_model)),
            _const_spec(w1.shape),
            _const_spec(w2.shape),
        ],
        out_specs=pl.BlockSpec((1, tm, d_model), lambda i, j: (i, j, 0)),
        out_shape=jax.ShapeDtypeStruct((b, s, d_model), F32),
        compiler_params=pltpu.CompilerParams(
            dimension_semantics=("arbitrary", "arbitrary"),
            vmem_limit_bytes=VMEM_LIMIT_BYTES),
        name="out_mlp",
    )(x, ya, yb, w_out.astype(BF16), norm_g.reshape(1, d_model), w1.astype(BF16), w2.astype(BF16))


def kernel(x, mem, norm_mix, norm_mem, w_mem_kv, xa_q_gain, xa_k_gain, w_out, norm_mlp, w_mlp_in, w_mlp_out, w_in_a, conv_w, conv_b, w_rgate, b_rgate, w_igate, b_igate, lru_lambda, w_in_b, b_forget, fox_q_gain, fox_k_gain):
    depth = norm_mix.shape[0]
    s = x.shape[1]
    ts = min(s, 512)
    t_attn = min(s, 512)
    k_mem, v_mem = _mem_kv(mem, norm_mem, w_mem_kv, xa_k_gain)
    for layer in range(depth):
        j = layer // 2
        if layer % 2 == 0:
            ya, yb = _lru_front(x, norm_mix[layer], w_in_a[j], conv_w[j], conv_b[j], w_rgate[j], b_rgate[j],
                                w_igate[j], b_igate[j], lru_lambda[j], k_mem[layer], v_mem[layer],
                                xa_q_gain[layer], ts)
        else:
            qt, kaug, vt, nb, yb = _fox_front(x, norm_mix[layer], w_in_b[j], b_forget[j], fox_q_gain[j],
                                              fox_k_gain[j], k_mem[layer], v_mem[layer], xa_q_gain[layer], ts)
            ya = _fox_attn(qt, kaug, vt, nb, fox_q_gain.shape[-1], t_attn, heads_per_step=4)
        x = _out_mlp(x, ya, yb, w_out[layer], norm_mlp[layer], w_mlp_in[layer], w_mlp_out[layer], ts,
                     ya_transposed=layer % 2 == 1)
    return x
```

```python
import functools
import math

import jax
import jax.numpy as jnp
from jax import lax
from jax.experimental import pallas as pl
from jax.experimental.pallas import tpu as pltpu

EPS = 1e-6
NEG_INF = -1e30
LRU_C = 8.0
CONV_WIDTH = 4
LOG2E = math.log2(math.e)
SUBLANES = 8
LANES = 128
BF16_ROWS = 16
MXU_TILE = 256
VMEM_LIMIT_BYTES = 56 * 1024 * 1024

F32 = jnp.float32
BF16 = jnp.bfloat16


def _rms(x):
    return x * lax.rsqrt(jnp.mean(x * x, axis=-1, keepdims=True) + EPS)


def _dot(a, b):
    return jnp.dot(a, b, preferred_element_type=F32)


def _dot_nt(a, b):
    return lax.dot_general(a, b, (((1,), (1,)), ((), ())), preferred_element_type=F32)


def _sigmoid(x):
    return 0.5 * jnp.tanh(0.5 * x) + 0.5


def _softplus(x):
    return jnp.maximum(x, 0.0) + jnp.log(1.0 + jnp.exp(-jnp.abs(x)))


def _gelu_tanh(x):
    c = math.sqrt(2.0 / math.pi)
    return x * (0.5 * (1.0 + jnp.tanh(c * (x + 0.044715 * (x * x * x)))))


def _const_spec(shape):
    nd = len(shape)
    return pl.BlockSpec(shape, lambda *_: (0,) * nd)


def _mem_kv_kernel(mem_ref, g_ref, w_ref, kg_ref, k_ref, v_ref, *, heads, hd):
    width = heads * hd
    hn = (_rms(mem_ref[0]) * g_ref[0]).astype(BF16)
    kv = _dot(hn, w_ref[0])
    for h in range(heads):
        kh = kv[:, h * hd:(h + 1) * hd]
        k_ref[0, 0, :, h * hd:(h + 1) * hd] = (_rms(kh) * kg_ref[0]).astype(BF16)
    v_ref[0, 0] = kv[:, width:].astype(BF16)


def _mem_kv(mem, norm_mem, w_mem_kv, xa_k_gain):
    depth, d_model, two_w = w_mem_kv.shape
    b, m, _ = mem.shape
    width = two_w // 2
    hd = xa_k_gain.shape[-1]
    heads = width // hd
    out = jax.ShapeDtypeStruct((depth, b, m, width), BF16)
    return pl.pallas_call(
        functools.partial(_mem_kv_kernel, heads=heads, hd=hd),
        grid=(depth, b),
        in_specs=[
            pl.BlockSpec((1, m, d_model), lambda l, i: (i, 0, 0)),
            pl.BlockSpec((1, 1, d_model), lambda l, i: (l, 0, 0)),
            pl.BlockSpec((1, d_model, two_w), lambda l, i: (l, 0, 0)),
            pl.BlockSpec((1, 1, hd), lambda l, i: (l, 0, 0)),
        ],
        out_specs=[
            pl.BlockSpec((1, 1, m, width), lambda l, i: (l, i, 0, 0)),
            pl.BlockSpec((1, 1, m, width), lambda l, i: (l, i, 0, 0)),
        ],
        out_shape=[out, out],
        compiler_params=pltpu.CompilerParams(
            dimension_semantics=("arbitrary", "arbitrary"),
            vmem_limit_bytes=VMEM_LIMIT_BYTES),
        name="mem_kv",
    )(mem, norm_mem.reshape(depth, 1, d_model), w_mem_kv.astype(BF16),
      xa_k_gain.reshape(depth, 1, hd))


def _mem_attn(xq, k_ref, v_ref, qg, heads, hd):
    scale = LOG2E / math.sqrt(hd)
    outs = []
    for h in range(heads):
        q = (_rms(xq[:, h * hd:(h + 1) * hd]) * (qg * scale)).astype(BF16)
        s = _dot_nt(q, k_ref[0, :, h * hd:(h + 1) * hd])
        p = jnp.exp2(s - jnp.max(s, axis=-1, keepdims=True))
        l = jnp.sum(p, axis=-1, keepdims=True)
        o = _dot(p.astype(BF16), v_ref[0, :, h * hd:(h + 1) * hd])
        outs.append(o / l)
    return jnp.concatenate(outs, axis=1)


def _lru_front_kernel(x_ref, g_ref, w_in_ref, cw_ref, cb_ref, wg_ref, br_ref, bi_ref,
                      lam_ref, k_ref, v_ref, qg_ref, y_ref, ym_ref,
                      u_tail, a_s, b_s, hc_ref, *, ts, pitch, d_lru, xa_heads, xa_hd):
    half = d_lru // 2

    @pl.when(pl.program_id(1) == 0)
    def _():
        u_tail[...] = jnp.zeros_like(u_tail)
        hc_ref[...] = jnp.zeros_like(hc_ref)

    h = (_rms(x_ref[0]) * g_ref[...]).astype(BF16)
    proj = _dot(h, w_in_ref[...])
    u = proj[:, :d_lru]
    gate = proj[:, d_lru:2 * d_lru]
    xq = proj[:, 2 * d_lru:]

    tail = u_tail[...]
    row8 = lax.broadcasted_iota(jnp.int32, (SUBLANES, d_lru), 0)
    uc = cb_ref[...] + cw_ref[CONV_WIDTH - 1:CONV_WIDTH, :] * u
    for d in range(1, CONV_WIDTH):
        rolled = pltpu.roll(u, d, 0)
        top = jnp.where(row8 < d, pltpu.roll(tail, d, 0), rolled[:SUBLANES])
        shifted = jnp.concatenate([top, rolled[SUBLANES:]], axis=0)
        uc = uc + cw_ref[CONV_WIDTH - 1 - d:CONV_WIDTH - d, :] * shifted
    u_tail[...] = u[ts - SUBLANES:, :]

    ub = uc.astype(BF16)
    g0 = _dot(ub[:, :half], wg_ref[0])
    g1 = _dot(ub[:, half:], wg_ref[1])
    r_pre = jnp.concatenate([g0[:, :half], g1[:, :half]], axis=1)
    i_pre = jnp.concatenate([g0[:, half:], g1[:, half:]], axis=1)
    r = _sigmoid(r_pre + br_ref[...])
    i = _sigmoid(i_pre + bi_ref[...])
    log_a = (-LRU_C) * r * _softplus(-lam_ref[...])
    a = jnp.exp(log_a)
    b = jnp.exp2((0.5 * LOG2E) * jnp.log(1.0 - a * a)) * (i * uc)

    seg = ts // SUBLANES
    nslab = d_lru // LANES
    for c in range(nslab):
        for sg in range(SUBLANES):
            a_s[c, pitch * sg:pitch * sg + seg, :] = a[seg * sg:seg * (sg + 1), LANES * c:LANES * (c + 1)]
            b_s[c, pitch * sg:pitch * sg + seg, :] = b[seg * sg:seg * (sg + 1), LANES * c:LANES * (c + 1)]

    def step(k, carry):
        out = []
        for c in range(nslab):
            rows = pl.ds(k, SUBLANES, stride=pitch)
            hv, pv = carry[c]
            av = a_s[c, rows, :]
            hv = av * hv + b_s[c, rows, :]
            pv = av * pv
            a_s[c, rows, :] = pv
            b_s[c, rows, :] = hv
            out.append((hv, pv))
        return tuple(out)

    zero = jnp.zeros((SUBLANES, LANES), F32)
    ends = lax.fori_loop(0, seg, step, ((zero, zero + 1.0),) * nslab, unroll=True)

    row8 = lax.broadcasted_iota(jnp.int32, (SUBLANES, LANES), 0)
    gl = _gelu_tanh(gate)
    for c in range(nslab):
        bv, av = ends[c]
        for d in (1, 2, 4):
            keep = row8 >= d
            a_sh = jnp.where(keep, pltpu.roll(av, d, 0), 1.0)
            b_sh = jnp.where(keep, pltpu.roll(bv, d, 0), 0.0)
            bv = av * b_sh + bv
            av = av * a_sh
        hc = hc_ref[:, LANES * c:LANES * (c + 1)]
        end_state = bv + av * hc
        enter = jnp.where(row8 == 0, hc, pltpu.roll(end_state, 1, 0))
        hc_ref[:, LANES * c:LANES * (c + 1)] = end_state[SUBLANES - 1:SUBLANES, :]
        for sg in range(SUBLANES):
            rows = slice(pitch * sg, pitch * sg + seg)
            hs = b_s[c, rows, :] + a_s[c, rows, :] * enter[sg:sg + 1, :]
            y_ref[0, seg * sg:seg * (sg + 1), LANES * c:LANES * (c + 1)] = (
                hs * gl[seg * sg:seg * (sg + 1), LANES * c:LANES * (c + 1)]).astype(BF16)
    ym_ref[0] = _mem_attn(xq, k_ref, v_ref, qg_ref[...], xa_heads, xa_hd).astype(BF16)


def _lru_front(x, norm_g, w_in, conv_w, conv_b, w_r, b_r, w_i, b_i, lam, k_mem, v_mem, xa_qg, ts):
    b, s, d_model = x.shape
    d_lru = conv_w.shape[-1]
    half = d_lru // 2
    xa_hd = xa_qg.shape[-1]
    xa_width = w_in.shape[1] - 2 * d_lru
    xa_heads = xa_width // xa_hd
    m = k_mem.shape[1]
    blk = w_r.shape[-1]
    nblk_half = half // blk
    wg = jnp.stack([
        jnp.concatenate([jax.scipy.linalg.block_diag(*w_r[hf * nblk_half:(hf + 1) * nblk_half]),
                         jax.scipy.linalg.block_diag(*w_i[hf * nblk_half:(hf + 1) * nblk_half])], axis=1)
        for hf in range(2)]).astype(BF16)
    seg = ts // SUBLANES
    pitch = seg + SUBLANES if (seg // SUBLANES) % 2 == 0 else seg + 2 * SUBLANES
    row = lambda v: v.reshape(1, -1)
    out = jax.ShapeDtypeStruct((b, s, d_lru), BF16)
    outm = jax.ShapeDtypeStruct((b, s, xa_width), BF16)
    return pl.pallas_call(
        functools.partial(_lru_front_kernel, ts=ts, pitch=pitch, d_lru=d_lru, xa_heads=xa_heads,
                          xa_hd=xa_hd),
        grid=(b, s // ts),
        in_specs=[
            pl.BlockSpec((1, ts, d_model), lambda i, j: (i, j, 0)),
            _const_spec((1, d_model)),
            _const_spec(w_in.shape),
            _const_spec(conv_w.shape),
            _const_spec((1, d_lru)),
            _const_spec(wg.shape),
            _const_spec((1, d_lru)),
            _const_spec((1, d_lru)),
            _const_spec((1, d_lru)),
            pl.BlockSpec((1, m, xa_width), lambda i, j: (i, 0, 0)),
            pl.BlockSpec((1, m, xa_width), lambda i, j: (i, 0, 0)),
            _const_spec((1, xa_hd)),
        ],
        out_specs=[
            pl.BlockSpec((1, ts, d_lru), lambda i, j: (i, j, 0)),
            pl.BlockSpec((1, ts, xa_width), lambda i, j: (i, j, 0)),
        ],
        out_shape=[out, outm],
        scratch_shapes=[
            pltpu.VMEM((SUBLANES, d_lru), F32),
            pltpu.VMEM((d_lru // LANES, SUBLANES * pitch, LANES), F32),
            pltpu.VMEM((d_lru // LANES, SUBLANES * pitch, LANES), F32),
            pltpu.VMEM((1, d_lru), F32),
        ],
        compiler_params=pltpu.CompilerParams(
            dimension_semantics=("arbitrary", "arbitrary"),
            vmem_limit_bytes=VMEM_LIMIT_BYTES),
        name="lru_front",
    )(x, row(norm_g), w_in.astype(BF16), conv_w, row(conv_b), wg, row(b_r), row(b_i), row(lam),
      k_mem, v_mem, row(xa_qg))


def _fox_front_kernel(x_ref, g_ref, w_ref, hm_ref, qg_ref, kg_ref, bf_ref, pc_ref, k_ref, v_ref,
                      xqg_ref, qt_out, ka_out, vt_out, nb_out, ym_ref, c_carry,
                      *, ts, width, heads, xa_heads, xa_hd):
    @pl.when(pl.program_id(1) == 0)
    def _():
        c_carry[...] = jnp.zeros_like(c_carry)

    h = (_rms(x_ref[0]) * g_ref[...]).astype(BF16)
    proj = _dot(h, w_ref[...])
    q = proj[:, :width]
    k = proj[:, width:2 * width]
    v = proj[:, 2 * width:3 * width]
    xq = proj[:, 3 * width:3 * width + xa_heads * xa_hd]
    f = proj[:, 3 * width + xa_heads * xa_hd:]

    def head_ms(t):
        t2 = (t * t).astype(BF16)
        return jnp.concatenate(
            [_dot(t2[:, c:c + MXU_TILE], hm_ref[...]) for c in range(0, width, MXU_TILE)], axis=1)

    qt_out[0] = jnp.transpose(q * lax.rsqrt(head_ms(q) + EPS) * qg_ref[...]).astype(BF16)
    vt_out[0] = jnp.transpose(v).astype(BF16)
    kn = (k * lax.rsqrt(head_ms(k) + EPS) * kg_ref[...]).astype(BF16)

    c = -_softplus(-(f + bf_ref[...]))
    row = lax.broadcasted_iota(jnp.int32, c.shape, 0)
    d = 1
    while d < ts:
        c = c + jnp.where(row >= d, pltpu.roll(c, d, 0), 0.0)
        d *= 2
    c = c + c_carry[...]
    c_carry[...] = c[ts - 1:ts, :]

    b2 = c * (-LOG2E)
    hi = b2.astype(BF16).astype(F32)
    r1 = b2 - hi
    mid = r1.astype(BF16).astype(F32)
    piece = pc_ref[...]
    pieces = jnp.where(piece == 0, hi, jnp.where(piece == 1, mid, r1 - mid))
    nb_out[0] = -jnp.transpose(pieces)[:SUBLANES * heads, :]
    lane = lax.broadcasted_iota(jnp.int32, (1, LANES), 1)
    one_lanes = (lane >= 4) & (lane < 7)
    for p in range(width // LANES):
        bias = pieces if p == 0 else pltpu.roll(pieces, LANES - 2 * SUBLANES * p, 1)
        ka_out[0, :, 2 * LANES * p:2 * LANES * p + LANES] = kn[:, LANES * p:LANES * (p + 1)]
        ka_out[0, :, 2 * LANES * p + LANES:2 * LANES * (p + 1)] = jnp.where(one_lanes, 1.0, bias).astype(BF16)

    ym_ref[0] = _mem_attn(xq, k_ref, v_ref, xqg_ref[...], xa_heads, xa_hd).astype(BF16)


def _fox_front(x, norm_g, w_in, b_f, q_gain, k_gain, k_mem, v_mem, xa_qg, ts):
    b, s, d_model = x.shape
    hd = q_gain.shape[-1]
    heads = b_f.shape[-1]
    width = heads * hd
    pairs = width // LANES
    xa_hd = xa_qg.shape[-1]
    xa_width = w_in.shape[1] - 3 * width - heads
    xa_heads = xa_width // xa_hd
    m = k_mem.shape[1]
    w_f = jnp.repeat(w_in[:, 3 * width:3 * width + heads], SUBLANES, axis=1)
    w_all = jnp.concatenate(
        [w_in[:, :3 * width], w_in[:, 3 * width + heads:], w_f,
         jnp.zeros((d_model, LANES - SUBLANES * heads), w_in.dtype)], axis=1).astype(BF16)
    hid = jnp.arange(MXU_TILE) // hd
    head_mean = jnp.where(hid[:, None] == hid[None, :], 1.0 / hd, 0.0).astype(BF16)
    piece = (jnp.arange(LANES, dtype=jnp.int32) % SUBLANES).reshape(1, LANES)
    qg = jnp.tile(q_gain * (LOG2E / math.sqrt(hd)), heads).reshape(1, width)
    kg = jnp.tile(k_gain, heads).reshape(1, width)
    bfp = jnp.zeros((1, LANES), F32).at[0, :SUBLANES * heads].set(jnp.repeat(b_f, SUBLANES))
    row = lambda t: t.reshape(1, -1)
    act_t = jax.ShapeDtypeStruct((b, width, s), BF16)
    return pl.pallas_call(
        functools.partial(_fox_front_kernel, ts=ts, width=width, heads=heads, xa_heads=xa_heads,
                          xa_hd=xa_hd),
        grid=(b, s // ts),
        in_specs=[
            pl.BlockSpec((1, ts, d_model), lambda i, j: (i, j, 0)),
            _const_spec((1, d_model)),
            _const_spec(w_all.shape),
            _const_spec(head_mean.shape),
            _const_spec((1, width)),
            _const_spec((1, width)),
            _const_spec((1, LANES)),
            _const_spec((1, LANES)),
            pl.BlockSpec((1, m, xa_width), lambda i, j: (i, 0, 0)),
            pl.BlockSpec((1, m, xa_width), lambda i, j: (i, 0, 0)),
            _const_spec((1, xa_hd)),
        ],
        out_specs=[
            pl.BlockSpec((1, width, ts), lambda i, j: (i, 0, j)),
            pl.BlockSpec((1, ts, 2 * width), lambda i, j: (i, j, 0)),
            pl.BlockSpec((1, width, ts), lambda i, j: (i, 0, j)),
            pl.BlockSpec((1, SUBLANES * heads, ts), lambda i, j: (i, 0, j)),
            pl.BlockSpec((1, ts, xa_width), lambda i, j: (i, j, 0)),
        ],
        out_shape=[act_t, jax.ShapeDtypeStruct((b, s, 2 * width), BF16), act_t,
                   jax.ShapeDtypeStruct((b, SUBLANES * heads, s), F32),
                   jax.ShapeDtypeStruct((b, s, xa_width), BF16)],
        scratch_shapes=[pltpu.VMEM((1, LANES), F32)],
        compiler_params=pltpu.CompilerParams(
            dimension_semantics=("arbitrary", "arbitrary"),
            vmem_limit_bytes=VMEM_LIMIT_BYTES),
        name="fox_front",
    )(x, row(norm_g), w_all, head_mean, qg, kg, bfp, piece, k_mem, v_mem, row(xa_qg))


def _fox_attn_kernel(qt_ref, qtn_ref, ka_ref, vt_ref, nb_ref, nbn_ref, o_ref,
                     s0_ref, s1_ref, sf_ref, m_ref, acc_ref, *, t, hd, heads, parts):
    i = pl.program_id(2)
    rowq = lax.broadcasted_iota(jnp.int32, (2 * hd, t), 0)
    row8 = lax.broadcasted_iota(jnp.int32, (SUBLANES, t), 0)
    keys = lax.broadcasted_iota(jnp.int32, (t, t), 0)
    queries = lax.broadcasted_iota(jnp.int32, (t, t), 1)
    tp = t // parts

    def weights(h, q_ref, b_ref):
        slot = h % 2
        qt = q_ref[0, 2 * hd * (h // 2):2 * hd * (h // 2 + 1), :]
        top = jnp.where((rowq >= hd * slot) & (rowq < hd * (slot + 1)), qt, jnp.zeros_like(qt))
        nb = pltpu.roll(b_ref[0, SUBLANES * h:SUBLANES * (h + 1), :], 4, 0)
        ones0 = jnp.where(row8 < 3, 1.0 if slot == 0 else 0.0, 0.0)
        ones1 = jnp.where(row8 < 3, 1.0 if slot == 1 else 0.0, 0.0)
        g0 = jnp.where((row8 >= 4) & (row8 < 7), nb, ones0)
        bot = jnp.concatenate([g0, ones1, jnp.zeros((2 * hd - 2 * SUBLANES, t), F32)], axis=0)
        return jnp.concatenate([top, bot.astype(BF16)], axis=0)

    w_here = [weights(h, qt_ref, nb_ref) for h in range(heads)]
    w_next = [weights(h, qtn_ref, nbn_ref) for h in range(heads)]

    def logits(j, s_ref, w, h, part):
        rows = pl.ds(pl.multiple_of(j * t, t) + tp * part, tp)
        ka = ka_ref[0, rows, 4 * hd * (h // 2):4 * hd * (h // 2 + 1)]
        s_ref[h, tp * part:tp * (part + 1), :] = _dot(ka, w[h]).astype(BF16)

    def block(jc, s_cur, masked, nxt):
        jn, s_next, w = nxt
        for h in range(heads):
            st = s_cur[h]
            if masked:
                st = jnp.where(keys <= queries, st, jnp.asarray(NEG_INF, BF16))
            m = m_ref[h]
            mn = jnp.maximum(m, jnp.max(st, axis=0, keepdims=True).astype(F32))
            m_ref[h] = mn
            mb = mn.astype(BF16)
            pv = None
            for part in range(parts):
                logits(jn, s_next, w, h, part)
                pt = jnp.exp2(st[tp * part:tp * (part + 1), :] - mb)
                cols = pl.ds(pl.multiple_of(jc * t, t) + tp * part, tp)
                vt = jnp.concatenate([vt_ref[0, hd * h:hd * (h + 1), cols], ones_row], axis=0)
                d = _dot(vt, pt)
                pv = d if pv is None else pv + d
            acc_ref[h] = jnp.exp2(m - mn) * acc_ref[h] + pv

    m_ref[...] = jnp.full(m_ref.shape, NEG_INF, F32)
    acc_ref[...] = jnp.zeros_like(acc_ref)
    ones_row = (lax.broadcasted_iota(jnp.int32, (BF16_ROWS, tp), 0) == 0).astype(BF16)

    first_of_next = (0, sf_ref, w_next)

    @pl.when(i == 0)
    def _():
        for h in range(heads):
            for part in range(parts):
                logits(0, sf_ref, w_here, h, part)
        block(0, sf_ref, True, first_of_next)

    @pl.when(i > 0)
    def _():
        block(0, sf_ref, False, (1, s1_ref, w_here))

    def two_blocks(jj, carry):
        j = 2 * jj + 1
        block(j, s1_ref, False, (j + 1, s0_ref, w_here))
        block(j + 1, s0_ref, False, (j + 2, s1_ref, w_here))
        return carry

    lax.fori_loop(0, lax.shift_right_logical(jnp.maximum(i - 1, 0), 1), two_blocks, 0)

    @pl.when(lax.bitwise_and(i, 1) == 1)
    def _():
        block(i, s1_ref, True, first_of_next)

    @pl.when((lax.bitwise_and(i, 1) == 0) & (i > 0))
    def _():
        block(i - 1, s1_ref, False, (i, s0_ref, w_here))
        block(i, s0_ref, True, first_of_next)

    for h in range(heads):
        o_ref[0, hd * h:hd * (h + 1), :] = (acc_ref[h, :hd, :] / acc_ref[h, hd:hd + 1, :]).astype(BF16)


def _fox_attn(qt, kaug, vt, nb, hd, t, heads_per_step):
    b, width, s = qt.shape
    hps = heads_per_step
    nq = s // t
    here = lambda bi, p, i: (bi, p, i)
    nxt = lambda bi, p, i: (bi, p, jnp.minimum(i + 1, nq - 1))
    return pl.pallas_call(
        functools.partial(_fox_attn_kernel, t=t, hd=hd, heads=hps, parts=2),
        grid=(b, width // (hps * hd), nq),
        in_specs=[
            pl.BlockSpec((1, hps * hd, t), here),
            pl.BlockSpec((1, hps * hd, t), nxt),
            pl.BlockSpec((1, s, 2 * hps * hd), lambda bi, p, i: (bi, 0, p)),
            pl.BlockSpec((1, hps * hd, s), lambda bi, p, i: (bi, p, 0)),
            pl.BlockSpec((1, hps * SUBLANES, t), here),
            pl.BlockSpec((1, hps * SUBLANES, t), nxt),
        ],
        out_specs=pl.BlockSpec((1, hps * hd, t), here),
        out_shape=jax.ShapeDtypeStruct((b, width, s), BF16),
        scratch_shapes=[
            pltpu.VMEM((hps, t, t), BF16),
            pltpu.VMEM((hps, t, t), BF16),
            pltpu.VMEM((hps, t, t), BF16),
            pltpu.VMEM((hps, 1, t), F32),
            pltpu.VMEM((hps, hd + BF16_ROWS, t), F32),
        ],
        compiler_params=pltpu.CompilerParams(
            dimension_semantics=("arbitrary", "arbitrary", "arbitrary"),
            vmem_limit_bytes=VMEM_LIMIT_BYTES),
        name="fox_attn",
    )(qt, qt, kaug, vt, nb, nb)


def _out_mlp_kernel(x_ref, ya_ref, yb_ref, wo_ref, g_ref, w1_ref, w2_ref, o_ref,
                    *, wa, ff_chunk, ya_transposed):
    if ya_transposed:
        mix = lax.dot_general(ya_ref[0], wo_ref[:wa, :], (((0,), (0,)), ((), ())),
                              preferred_element_type=F32)
    else:
        mix = _dot(ya_ref[0], wo_ref[:wa, :])
    x1 = x_ref[0] + mix + _dot(yb_ref[0], wo_ref[wa:, :])
    h = (_rms(x1) * g_ref[...]).astype(BF16)
    d_ff = w1_ref.shape[1]
    o_ref[0] = x1
    for c0 in range(0, d_ff, ff_chunk):
        a = jnp.maximum(_dot(h, w1_ref[:, c0:c0 + ff_chunk]), 0.0)
        o_ref[0] += _dot((a * a).astype(BF16), w2_ref[c0:c0 + ff_chunk, :])


def _out_mlp(x, ya, yb, w_out, norm_g, w1, w2, tm, ya_transposed):
    b, s, d_model = x.shape
    wb = yb.shape[-1]
    wa = w_out.shape[0] - wb
    d_ff = w1.shape[1]
    if ya_transposed:
        ya_spec = pl.BlockSpec((1, wa, tm), lambda i, j: (i, 0, j))
    else:
        ya_spec = pl.BlockSpec((1, tm, wa), lambda i, j: (i, j, 0))
    return pl.pallas_call(
        functools.partial(_out_mlp_kernel, wa=wa, ff_chunk=min(d_ff, 1024), ya_transposed=ya_transposed),
        grid=(b, s // tm),
        in_specs=[
            pl.BlockSpec((1, tm, d_model), lambda i, j: (i, j, 0)),
            ya_spec,
            pl.BlockSpec((1, tm, wb), lambda i, j: (i, j, 0)),
            _const_spec(w_out.shape),
            _const_spec((1, d_model)),
            _const_spec(w1.shape),
            _const_spec(w2.shape),
        ],
        out_specs=pl.BlockSpec((1, tm, d_model), lambda i, j: (i, j, 0)),
        out_shape=jax.ShapeDtypeStruct((b, s, d_model), F32),
        compiler_params=pltpu.CompilerParams(
            dimension_semantics=("arbitrary", "arbitrary"),
            vmem_limit_bytes=VMEM_LIMIT_BYTES),
        name="out_mlp",
    )(x, ya, yb, w_out.astype(BF16), norm_g.reshape(1, d_model), w1.astype(BF16), w2.astype(BF16))


def kernel(x, mem, norm_mix, norm_mem, w_mem_kv, xa_q_gain, xa_k_gain, w_out, norm_mlp, w_mlp_in, w_mlp_out, w_in_a, conv_w, conv_b, w_rgate, b_rgate, w_igate, b_igate, lru_lambda, w_in_b, b_forget, fox_q_gain, fox_k_gain):
    depth = norm_mix.shape[0]
    s = x.shape[1]
    ts = min(s, 512)
    t_attn = min(s, 512)
    k_mem, v_mem = _mem_kv(mem, norm_mem, w_mem_kv, xa_k_gain)
    for layer in range(depth):
        j = layer // 2
        if layer % 2 == 0:
            ya, yb = _lru_front(x, norm_mix[layer], w_in_a[j], conv_w[j], conv_b[j], w_rgate[j], b_rgate[j],
                                w_igate[j], b_igate[j], lru_lambda[j], k_mem[layer], v_mem[layer],
                                xa_q_gain[layer], ts)
        else:
            qt, kaug, vt, nb, yb = _fox_front(x, norm_mix[layer], w_in_b[j], b_forget[j], fox_q_gain[j],
                                              fox_k_gain[j], k_mem[layer], v_mem[layer], xa_q_gain[layer], ts)
            ya = _fox_attn(qt, kaug, vt, nb, fox_q_gain.shape[-1], t_attn, heads_per_step=4)
        x = _out_mlp(x, ya, yb, w_out[layer], norm_mlp[layer], w_mlp_in[layer], w_mlp_out[layer], ts,
                     ya_transposed=layer % 2 == 1)
    return x
```

```python
import functools
import math

import jax
import jax.numpy as jnp
from jax import lax
from jax.experimental import pallas as pl
from jax.experimental.pallas import tpu as pltpu

EPS = 1e-6
NEG_INF = -1e30
LRU_C = 8.0
CONV_WIDTH = 4
LOG2E = math.log2(math.e)
SUBLANES = 8
LANES = 128
BF16_ROWS = 16
MXU_TILE = 256
VMEM_LIMIT_BYTES = 56 * 1024 * 1024

F32 = jnp.float32
BF16 = jnp.bfloat16


def _rms(x):
    return x * lax.rsqrt(jnp.mean(x * x, axis=-1, keepdims=True) + EPS)


def _dot(a, b):
    return jnp.dot(a, b, preferred_element_type=F32)


def _dot_nt(a, b):
    return lax.dot_general(a, b, (((1,), (1,)), ((), ())), preferred_element_type=F32)


def _softplus(x):
    return jnp.maximum(x, 0.0) + jnp.log(1.0 + jnp.exp(-jnp.abs(x)))


def _gelu_tanh(x):
    c = math.sqrt(2.0 / math.pi)
    return x * (0.5 * (1.0 + jnp.tanh(c * (x + 0.044715 * (x * x * x)))))


def _const_spec(shape):
    nd = len(shape)
    return pl.BlockSpec(shape, lambda *_: (0,) * nd)


def _mem_kv_kernel(mem_ref, g_ref, w_ref, kg_ref, k_ref, v_ref, *, heads, hd):
    width = heads * hd
    hn = (_rms(mem_ref[0]) * g_ref[0]).astype(BF16)
    kv = _dot(hn, w_ref[0])
    for h in range(heads):
        kh = kv[:, h * hd:(h + 1) * hd]
        k_ref[0, 0, :, h * hd:(h + 1) * hd] = (_rms(kh) * kg_ref[0]).astype(BF16)
    v_ref[0, 0] = kv[:, width:].astype(BF16)


def _mem_kv(mem, norm_mem, w_mem_kv, xa_k_gain):
    depth, d_model, two_w = w_mem_kv.shape
    b, m, _ = mem.shape
    width = two_w // 2
    hd = xa_k_gain.shape[-1]
    heads = width // hd
    out = jax.ShapeDtypeStruct((depth, b, m, width), BF16)
    return pl.pallas_call(
        functools.partial(_mem_kv_kernel, heads=heads, hd=hd),
        grid=(depth, b),
        in_specs=[
            pl.BlockSpec((1, m, d_model), lambda l, i: (i, 0, 0)),
            pl.BlockSpec((1, 1, d_model), lambda l, i: (l, 0, 0)),
            pl.BlockSpec((1, d_model, two_w), lambda l, i: (l, 0, 0)),
            pl.BlockSpec((1, 1, hd), lambda l, i: (l, 0, 0)),
        ],
        out_specs=[
            pl.BlockSpec((1, 1, m, width), lambda l, i: (l, i, 0, 0)),
            pl.BlockSpec((1, 1, m, width), lambda l, i: (l, i, 0, 0)),
        ],
        out_shape=[out, out],
        compiler_params=pltpu.CompilerParams(
            dimension_semantics=("arbitrary", "arbitrary"),
            vmem_limit_bytes=VMEM_LIMIT_BYTES),
        name="mem_kv",
    )(mem, norm_mem.reshape(depth, 1, d_model), w_mem_kv.astype(BF16),
      xa_k_gain.reshape(depth, 1, hd))


def _mem_attn(xq, k_ref, v_ref, qg, heads, hd):
    scale = LOG2E / math.sqrt(hd)
    outs = []
    for h in range(heads):
        q = (_rms(xq[:, h * hd:(h + 1) * hd]) * (qg * scale)).astype(BF16)
        s = _dot_nt(q, k_ref[0, :, h * hd:(h + 1) * hd])
        p = jnp.exp2(s - jnp.max(s, axis=-1, keepdims=True))
        l = jnp.sum(p, axis=-1, keepdims=True)
        o = _dot(p.astype(BF16), v_ref[0, :, h * hd:(h + 1) * hd])
        outs.append(o / l)
    return jnp.concatenate(outs, axis=1)


def _lru_front_kernel(x_ref, g_ref, w_in_ref, cw_ref, cb_ref, wg_ref, br_ref, bi_ref,
                      lam_ref, k_ref, v_ref, qg_ref, y_ref, ym_ref,
                      u_tail, a_s, b_s, hc_ref, *, ts, pitch, d_lru, xa_heads, xa_hd):
    half = d_lru // 2

    @pl.when(pl.program_id(1) == 0)
    def _():
        u_tail[...] = jnp.zeros_like(u_tail)
        hc_ref[...] = jnp.zeros_like(hc_ref)

    h = (_rms(x_ref[0]) * g_ref[...]).astype(BF16)
    proj = _dot(h, w_in_ref[...])
    u = proj[:, :d_lru]
    gate = proj[:, d_lru:2 * d_lru]
    xq = proj[:, 2 * d_lru:]

    tail = u_tail[...]
    row8 = lax.broadcasted_iota(jnp.int32, (SUBLANES, d_lru), 0)
    uc = cb_ref[...] + cw_ref[CONV_WIDTH - 1:CONV_WIDTH, :] * u
    for d in range(1, CONV_WIDTH):
        rolled = pltpu.roll(u, d, 0)
        top = jnp.where(row8 < d, pltpu.roll(tail, d, 0), rolled[:SUBLANES])
        shifted = jnp.concatenate([top, rolled[SUBLANES:]], axis=0)
        uc = uc + cw_ref[CONV_WIDTH - 1 - d:CONV_WIDTH - d, :] * shifted
    u_tail[...] = u[ts - SUBLANES:, :]

    ub = uc.astype(BF16)
    g0 = _dot(ub[:, :half], wg_ref[0])
    g1 = _dot(ub[:, half:], wg_ref[1])
    r_pre = jnp.concatenate([g0[:, :half], g1[:, :half]], axis=1)
    i_pre = jnp.concatenate([g0[:, half:], g1[:, half:]], axis=1)
    n = (-0.5 * LRU_C * LOG2E) * _softplus(-lam_ref[...])
    a = jnp.exp2(n * jnp.tanh(r_pre + br_ref[...]) + n)
    i = 0.5 * jnp.tanh(i_pre + bi_ref[...]) + 0.5
    b = jnp.exp2((0.5 * LOG2E) * jnp.log(1.0 - a * a)) * (i * uc)

    seg = ts // SUBLANES
    nslab = d_lru // LANES
    for c in range(nslab):
        for sg in range(SUBLANES):
            a_s[c, pitch * sg:pitch * sg + seg, :] = a[seg * sg:seg * (sg + 1), LANES * c:LANES * (c + 1)]
            b_s[c, pitch * sg:pitch * sg + seg, :] = b[seg * sg:seg * (sg + 1), LANES * c:LANES * (c + 1)]

    def step(k, carry):
        out = []
        for c in range(nslab):
            rows = pl.ds(k, SUBLANES, stride=pitch)
            hv, pv = carry[c]
            av = a_s[c, rows, :]
            hv = av * hv + b_s[c, rows, :]
            pv = av * pv
            a_s[c, rows, :] = pv
            b_s[c, rows, :] = hv
            out.append((hv, pv))
        return tuple(out)

    zero = jnp.zeros((SUBLANES, LANES), F32)
    ends = lax.fori_loop(0, seg, step, ((zero, zero + 1.0),) * nslab, unroll=True)

    row8 = lax.broadcasted_iota(jnp.int32, (SUBLANES, LANES), 0)
    gl = _gelu_tanh(gate)
    for c in range(nslab):
        bv, av = ends[c]
        for d in (1, 2, 4):
            keep = row8 >= d
            a_sh = jnp.where(keep, pltpu.roll(av, d, 0), 1.0)
            b_sh = jnp.where(keep, pltpu.roll(bv, d, 0), 0.0)
            bv = av * b_sh + bv
            av = av * a_sh
        hc = hc_ref[:, LANES * c:LANES * (c + 1)]
        end_state = bv + av * hc
        enter = jnp.where(row8 == 0, hc, pltpu.roll(end_state, 1, 0))
        hc_ref[:, LANES * c:LANES * (c + 1)] = end_state[SUBLANES - 1:SUBLANES, :]
        for sg in range(SUBLANES):
            rows = slice(pitch * sg, pitch * sg + seg)
            hs = b_s[c, rows, :] + a_s[c, rows, :] * enter[sg:sg + 1, :]
            y_ref[0, seg * sg:seg * (sg + 1), LANES * c:LANES * (c + 1)] = (
                hs * gl[seg * sg:seg * (sg + 1), LANES * c:LANES * (c + 1)]).astype(BF16)
    ym_ref[0] = _mem_attn(xq, k_ref, v_ref, qg_ref[...], xa_heads, xa_hd).astype(BF16)


def _lru_front(x, norm_g, w_in, conv_w, conv_b, w_r, b_r, w_i, b_i, lam, k_mem, v_mem, xa_qg, ts):
    b, s, d_model = x.shape
    d_lru = conv_w.shape[-1]
    half = d_lru // 2
    xa_hd = xa_qg.shape[-1]
    xa_width = w_in.shape[1] - 2 * d_lru
    xa_heads = xa_width // xa_hd
    m = k_mem.shape[1]
    blk = w_r.shape[-1]
    nblk_half = half // blk
    wg = (0.5 * jnp.stack([
        jnp.concatenate([jax.scipy.linalg.block_diag(*w_r[hf * nblk_half:(hf + 1) * nblk_half]),
                         jax.scipy.linalg.block_diag(*w_i[hf * nblk_half:(hf + 1) * nblk_half])], axis=1)
        for hf in range(2)])).astype(BF16)
    b_r = 0.5 * b_r
    b_i = 0.5 * b_i
    seg = ts // SUBLANES
    pitch = seg + SUBLANES if (seg // SUBLANES) % 2 == 0 else seg + 2 * SUBLANES
    row = lambda v: v.reshape(1, -1)
    out = jax.ShapeDtypeStruct((b, s, d_lru), BF16)
    outm = jax.ShapeDtypeStruct((b, s, xa_width), BF16)
    return pl.pallas_call(
        functools.partial(_lru_front_kernel, ts=ts, pitch=pitch, d_lru=d_lru, xa_heads=xa_heads,
                          xa_hd=xa_hd),
        grid=(b, s // ts),
        in_specs=[
            pl.BlockSpec((1, ts, d_model), lambda i, j: (i, j, 0)),
            _const_spec((1, d_model)),
            _const_spec(w_in.shape),
            _const_spec(conv_w.shape),
            _const_spec((1, d_lru)),
            _const_spec(wg.shape),
            _const_spec((1, d_lru)),
            _const_spec((1, d_lru)),
            _const_spec((1, d_lru)),
            pl.BlockSpec((1, m, xa_width), lambda i, j: (i, 0, 0)),
            pl.BlockSpec((1, m, xa_width), lambda i, j: (i, 0, 0)),
            _const_spec((1, xa_hd)),
        ],
        out_specs=[
            pl.BlockSpec((1, ts, d_lru), lambda i, j: (i, j, 0)),
            pl.BlockSpec((1, ts, xa_width), lambda i, j: (i, j, 0)),
        ],
        out_shape=[out, outm],
        scratch_shapes=[
            pltpu.VMEM((SUBLANES, d_lru), F32),
            pltpu.VMEM((d_lru // LANES, SUBLANES * pitch, LANES), F32),
            pltpu.VMEM((d_lru // LANES, SUBLANES * pitch, LANES), F32),
            pltpu.VMEM((1, d_lru), F32),
        ],
        compiler_params=pltpu.CompilerParams(
            dimension_semantics=("arbitrary", "arbitrary"),
            vmem_limit_bytes=VMEM_LIMIT_BYTES),
        name="lru_front",
    )(x, row(norm_g), w_in.astype(BF16), conv_w, row(conv_b), wg, row(b_r), row(b_i), row(lam),
      k_mem, v_mem, row(xa_qg))


def _fox_front_kernel(x_ref, g_ref, w_ref, hm_ref, qg_ref, kg_ref, bf_ref, pc_ref, k_ref, v_ref,
                      xqg_ref, qt_out, ka_out, vt_out, nb_out, ym_ref, c_carry,
                      *, ts, sub, width, heads, xa_heads, xa_hd):
    @pl.when(pl.program_id(1) == 0)
    def _():
        c_carry[...] = jnp.zeros_like(c_carry)

    xa_width = xa_heads * xa_hd
    col_blocks = [(0, width), (width, 2 * width), (2 * width, 3 * width),
                  (3 * width, 3 * width + xa_width), (3 * width + xa_width, w_ref.shape[1])]

    def project(r0, out):
        h = (_rms(x_ref[0, r0:r0 + sub, :]) * g_ref[...]).astype(BF16)
        for c0, c1 in col_blocks:
            out.append(_dot(h, w_ref[:, c0:c1]))
            yield

    def head_ms(t):
        t2 = (t * t).astype(BF16)
        return jnp.concatenate(
            [_dot(t2[:, c:c + MXU_TILE], hm_ref[...]) for c in range(0, width, MXU_TILE)], axis=1)

    lane = lax.broadcasted_iota(jnp.int32, (1, LANES), 1)
    one_lanes = (lane >= 4) & (lane < 7)
    row = lax.broadcasted_iota(jnp.int32, (sub, LANES), 0)

    def finish(r0, proj):
        q, k, v, xq, f = proj
        qt_out[0, :, r0:r0 + sub] = jnp.transpose(
            q * lax.rsqrt(head_ms(q) + EPS) * qg_ref[...]).astype(BF16)
        yield
        vt_out[0, :, r0:r0 + sub] = jnp.transpose(v).astype(BF16)
        kn = (k * lax.rsqrt(head_ms(k) + EPS) * kg_ref[...]).astype(BF16)
        yield
        c = -_softplus(-(f + bf_ref[...]))
        d = 1
        while d < sub:
            c = c + jnp.where(row >= d, pltpu.roll(c, d, 0), 0.0)
            d *= 2
        c = c + c_carry[...]
        c_carry[...] = c[sub - 1:sub, :]
        b2 = c * (-LOG2E)
        hi = b2.astype(BF16).astype(F32)
        r1 = b2 - hi
        mid = r1.astype(BF16).astype(F32)
        piece = pc_ref[...]
        pieces = jnp.where(piece == 0, hi, jnp.where(piece == 1, mid, r1 - mid))
        nb_out[0, :, r0:r0 + sub] = -jnp.transpose(pieces)[:SUBLANES * heads, :]
        for p in range(width // LANES):
            bias = pieces if p == 0 else pltpu.roll(pieces, LANES - 2 * SUBLANES * p, 1)
            ka_out[0, r0:r0 + sub, 2 * LANES * p:2 * LANES * p + LANES] = kn[:, LANES * p:LANES * (p + 1)]
            ka_out[0, r0:r0 + sub, 2 * LANES * p + LANES:2 * LANES * (p + 1)] = (
                jnp.where(one_lanes, 1.0, bias).astype(BF16))
        yield
        ym_ref[0, r0:r0 + sub, :] = _mem_attn(xq, k_ref, v_ref, xqg_ref[...], xa_heads, xa_hd).astype(BF16)
        yield

    proj = []
    for _ in project(0, proj):
        pass
    for r0 in range(0, ts, sub):
        nxt = []
        ahead = project(r0 + sub, nxt) if r0 + sub < ts else iter(())
        for _ in finish(r0, proj):
            next(ahead, None)
        for _ in ahead:
            pass
        proj = nxt


def _fox_front(x, norm_g, w_in, b_f, q_gain, k_gain, k_mem, v_mem, xa_qg, ts):
    b, s, d_model = x.shape
    hd = q_gain.shape[-1]
    heads = b_f.shape[-1]
    width = heads * hd
    pairs = width // LANES
    xa_hd = xa_qg.shape[-1]
    xa_width = w_in.shape[1] - 3 * width - heads
    xa_heads = xa_width // xa_hd
    m = k_mem.shape[1]
    w_f = jnp.repeat(w_in[:, 3 * width:3 * width + heads], SUBLANES, axis=1)
    w_all = jnp.concatenate(
        [w_in[:, :3 * width], w_in[:, 3 * width + heads:], w_f,
         jnp.zeros((d_model, LANES - SUBLANES * heads), w_in.dtype)], axis=1).astype(BF16)
    hid = jnp.arange(MXU_TILE) // hd
    head_mean = jnp.where(hid[:, None] == hid[None, :], 1.0 / hd, 0.0).astype(BF16)
    piece = (jnp.arange(LANES, dtype=jnp.int32) % SUBLANES).reshape(1, LANES)
    qg = jnp.tile(q_gain * (LOG2E / math.sqrt(hd)), heads).reshape(1, width)
    kg = jnp.tile(k_gain, heads).reshape(1, width)
    bfp = jnp.zeros((1, LANES), F32).at[0, :SUBLANES * heads].set(jnp.repeat(b_f, SUBLANES))
    row = lambda t: t.reshape(1, -1)
    act_t = jax.ShapeDtypeStruct((b, width, s), BF16)
    return pl.pallas_call(
        functools.partial(_fox_front_kernel, ts=ts, sub=min(ts, MXU_TILE), width=width, heads=heads,
                          xa_heads=xa_heads, xa_hd=xa_hd),
        grid=(b, s // ts),
        in_specs=[
            pl.BlockSpec((1, ts, d_model), lambda i, j: (i, j, 0)),
            _const_spec((1, d_model)),
            _const_spec(w_all.shape),
            _const_spec(head_mean.shape),
            _const_spec((1, width)),
            _const_spec((1, width)),
            _const_spec((1, LANES)),
            _const_spec((1, LANES)),
            pl.BlockSpec((1, m, xa_width), lambda i, j: (i, 0, 0)),
            pl.BlockSpec((1, m, xa_width), lambda i, j: (i, 0, 0)),
            _const_spec((1, xa_hd)),
        ],
        out_specs=[
            pl.BlockSpec((1, width, ts), lambda i, j: (i, 0, j)),
            pl.BlockSpec((1, ts, 2 * width), lambda i, j: (i, j, 0)),
            pl.BlockSpec((1, width, ts), lambda i, j: (i, 0, j)),
            pl.BlockSpec((1, SUBLANES * heads, ts), lambda i, j: (i, 0, j)),
            pl.BlockSpec((1, ts, xa_width), lambda i, j: (i, j, 0)),
        ],
        out_shape=[act_t, jax.ShapeDtypeStruct((b, s, 2 * width), BF16), act_t,
                   jax.ShapeDtypeStruct((b, SUBLANES * heads, s), F32),
                   jax.ShapeDtypeStruct((b, s, xa_width), BF16)],
        scratch_shapes=[pltpu.VMEM((1, LANES), F32)],
        compiler_params=pltpu.CompilerParams(
            dimension_semantics=("arbitrary", "arbitrary"),
            vmem_limit_bytes=VMEM_LIMIT_BYTES),
        name="fox_front",
    )(x, row(norm_g), w_all, head_mean, qg, kg, bfp, piece, k_mem, v_mem, row(xa_qg))


def _fox_attn_kernel(qt_ref, qtn_ref, ka_ref, vt_ref, nb_ref, nbn_ref, o_ref,
                     s0_ref, s1_ref, sf_ref, m_ref, acc_ref, *, t, hd, heads, parts):
    i = pl.program_id(2)
    rowq = lax.broadcasted_iota(jnp.int32, (2 * hd, t), 0)
    row8 = lax.broadcasted_iota(jnp.int32, (SUBLANES, t), 0)
    keys = lax.broadcasted_iota(jnp.int32, (t, t), 0)
    queries = lax.broadcasted_iota(jnp.int32, (t, t), 1)
    tp = t // parts

    def weights(h, q_ref, b_ref):
        slot = h % 2
        qt = q_ref[0, 2 * hd * (h // 2):2 * hd * (h // 2 + 1), :]
        top = jnp.where((rowq >= hd * slot) & (rowq < hd * (slot + 1)), qt, jnp.zeros_like(qt))
        nb = pltpu.roll(b_ref[0, SUBLANES * h:SUBLANES * (h + 1), :], 4, 0)
        ones0 = jnp.where(row8 < 3, 1.0 if slot == 0 else 0.0, 0.0)
        ones1 = jnp.where(row8 < 3, 1.0 if slot == 1 else 0.0, 0.0)
        g0 = jnp.where((row8 >= 4) & (row8 < 7), nb, ones0)
        bot = jnp.concatenate([g0, ones1, jnp.zeros((2 * hd - 2 * SUBLANES, t), F32)], axis=0)
        return jnp.concatenate([top, bot.astype(BF16)], axis=0)

    w_here = [weights(h, qt_ref, nb_ref) for h in range(heads)]
    w_next = [weights(h, qtn_ref, nbn_ref) for h in range(heads)]

    def logits(j, s_ref, w, h, part):
        rows = pl.ds(pl.multiple_of(j * t, t) + tp * part, tp)
        ka = ka_ref[0, rows, 4 * hd * (h // 2):4 * hd * (h // 2 + 1)]
        s_ref[h, tp * part:tp * (part + 1), :] = _dot(ka, w[h]).astype(BF16)

    def block(jc, s_cur, masked, nxt):
        jn, s_next, w = nxt
        for h in range(heads):
            st = s_cur[h]
            if masked:
                st = jnp.where(keys <= queries, st, jnp.asarray(NEG_INF, BF16))
            m = m_ref[h]
            mn = jnp.maximum(m, jnp.max(st, axis=0, keepdims=True).astype(F32))
            m_ref[h] = mn
            mb = mn.astype(BF16)
            pv = None
            for part in range(parts):
                logits(jn, s_next, w, h, part)
                pt = jnp.exp2(st[tp * part:tp * (part + 1), :] - mb)
                cols = pl.ds(pl.multiple_of(jc * t, t) + tp * part, tp)
                vt = jnp.concatenate([vt_ref[0, hd * h:hd * (h + 1), cols], ones_row], axis=0)
                d = _dot(vt, pt)
                pv = d if pv is None else pv + d
            acc_ref[h] = jnp.exp2(m - mn) * acc_ref[h] + pv

    m_ref[...] = jnp.full(m_ref.shape, NEG_INF, F32)
    acc_ref[...] = jnp.zeros_like(acc_ref)
    ones_row = (lax.broadcasted_iota(jnp.int32, (BF16_ROWS, tp), 0) == 0).astype(BF16)

    first_of_next = (0, sf_ref, w_next)

    @pl.when(i == 0)
    def _():
        for h in range(heads):
            for part in range(parts):
                logits(0, sf_ref, w_here, h, part)
        block(0, sf_ref, True, first_of_next)

    @pl.when(i > 0)
    def _():
        block(0, sf_ref, False, (1, s1_ref, w_here))

    def two_blocks(jj, carry):
        j = 2 * jj + 1
        block(j, s1_ref, False, (j + 1, s0_ref, w_here))
        block(j + 1, s0_ref, False, (j + 2, s1_ref, w_here))
        return carry

    lax.fori_loop(0, lax.shift_right_logical(jnp.maximum(i - 1, 0), 1), two_blocks, 0)

    @pl.when(lax.bitwise_and(i, 1) == 1)
    def _():
        block(i, s1_ref, True, first_of_next)

    @pl.when((lax.bitwise_and(i, 1) == 0) & (i > 0))
    def _():
        block(i - 1, s1_ref, False, (i, s0_ref, w_here))
        block(i, s0_ref, True, first_of_next)

    for h in range(heads):
        o_ref[0, hd * h:hd * (h + 1), :] = (acc_ref[h, :hd, :] / acc_ref[h, hd:hd + 1, :]).astype(BF16)


def _fox_attn(qt, kaug, vt, nb, hd, t, heads_per_step):
    b, width, s = qt.shape
    hps = heads_per_step
    nq = s // t
    here = lambda bi, p, i: (bi, p, i)
    nxt = lambda bi, p, i: (bi, p, jnp.minimum(i + 1, nq - 1))
    return pl.pallas_call(
        functools.partial(_fox_attn_kernel, t=t, hd=hd, heads=hps, parts=2),
        grid=(b, width // (hps * hd), nq),
        in_specs=[
            pl.BlockSpec((1, hps * hd, t), here),
            pl.BlockSpec((1, hps * hd, t), nxt),
            pl.BlockSpec((1, s, 2 * hps * hd), lambda bi, p, i: (bi, 0, p)),
            pl.BlockSpec((1, hps * hd, s), lambda bi, p, i: (bi, p, 0)),
            pl.BlockSpec((1, hps * SUBLANES, t), here),
            pl.BlockSpec((1, hps * SUBLANES, t), nxt),
        ],
        out_specs=pl.BlockSpec((1, hps * hd, t), here),
        out_shape=jax.ShapeDtypeStruct((b, width, s), BF16),
        scratch_shapes=[
            pltpu.VMEM((hps, t, t), BF16),
            pltpu.VMEM((hps, t, t), BF16),
            pltpu.VMEM((hps, t, t), BF16),
            pltpu.VMEM((hps, 1, t), F32),
            pltpu.VMEM((hps, hd + BF16_ROWS, t), F32),
        ],
        compiler_params=pltpu.CompilerParams(
            dimension_semantics=("arbitrary", "arbitrary", "arbitrary"),
            vmem_limit_bytes=VMEM_LIMIT_BYTES),
        name="fox_attn",
    )(qt, qt, kaug, vt, nb, nb)


def _out_mlp_kernel(x_ref, ya_ref, yb_ref, wo_ref, g_ref, w1_ref, w2_ref, o_ref,
                    *, wa, ff_chunk, ya_transposed):
    if ya_transposed:
        mix = lax.dot_general(ya_ref[0], wo_ref[:wa, :], (((0,), (0,)), ((), ())),
                              preferred_element_type=F32)
    else:
        mix = _dot(ya_ref[0], wo_ref[:wa, :])
    x1 = x_ref[0] + mix + _dot(yb_ref[0], wo_ref[wa:, :])
    h = (_rms(x1) * g_ref[...]).astype(BF16)
    d_ff = w1_ref.shape[1]
    o_ref[0] = x1
    for c0 in range(0, d_ff, ff_chunk):
        a = jnp.maximum(_dot(h, w1_ref[:, c0:c0 + ff_chunk]), 0.0)
        o_ref[0] += _dot((a * a).astype(BF16), w2_ref[c0:c0 + ff_chunk, :])


def _out_mlp(x, ya, yb, w_out, norm_g, w1, w2, tm, ya_transposed):
    b, s, d_model = x.shape
    wb = yb.shape[-1]
    wa = w_out.shape[0] - wb
    d_ff = w1.shape[1]
    if ya_transposed:
        ya_spec = pl.BlockSpec((1, wa, tm), lambda i, j: (i, 0, j))
    else:
        ya_spec = pl.BlockSpec((1, tm, wa), lambda i, j: (i, j, 0))
    return pl.pallas_call(
        functools.partial(_out_mlp_kernel, wa=wa, ff_chunk=min(d_ff, 1024), ya_transposed=ya_transposed),
        grid=(b, s // tm),
        in_specs=[
            pl.BlockSpec((1, tm, d_model), lambda i, j: (i, j, 0)),
            ya_spec,
            pl.BlockSpec((1, tm, wb), lambda i, j: (i, j, 0)),
            _const_spec(w_out.shape),
            _const_spec((1, d_model)),
            _const_spec(w1.shape),
            _const_spec(w2.shape),
        ],
        out_specs=pl.BlockSpec((1, tm, d_model), lambda i, j: (i, j, 0)),
        out_shape=jax.ShapeDtypeStruct((b, s, d_model), F32),
        compiler_params=pltpu.CompilerParams(
            dimension_semantics=("arbitrary", "arbitrary"),
            vmem_limit_bytes=VMEM_LIMIT_BYTES),
        name="out_mlp",
    )(x, ya, yb, w_out.astype(BF16), norm_g.reshape(1, d_model), w1.astype(BF16), w2.astype(BF16))


def kernel(x, mem, norm_mix, norm_mem, w_mem_kv, xa_q_gain, xa_k_gain, w_out, norm_mlp, w_mlp_in, w_mlp_out, w_in_a, conv_w, conv_b, w_rgate, b_rgate, w_igate, b_igate, lru_lambda, w_in_b, b_forget, fox_q_gain, fox_k_gain):
    depth = norm_mix.shape[0]
    s = x.shape[1]
    ts = min(s, 512)
    ts_front = min(s, 1024)
    t_attn = min(s, 512)
    k_mem, v_mem = _mem_kv(mem, norm_mem, w_mem_kv, xa_k_gain)
    for layer in range(depth):
        j = layer // 2
        if layer % 2 == 0:
            ya, yb = _lru_front(x, norm_mix[layer], w_in_a[j], conv_w[j], conv_b[j], w_rgate[j], b_rgate[j],
                                w_igate[j], b_igate[j], lru_lambda[j], k_mem[layer], v_mem[layer],
                                xa_q_gain[layer], ts_front)
        else:
            qt, kaug, vt, nb, yb = _fox_front(x, norm_mix[layer], w_in_b[j], b_forget[j], fox_q_gain[j],
                                              fox_k_gain[j], k_mem[layer], v_mem[layer], xa_q_gain[layer],
                                              ts_front)
            ya = _fox_attn(qt, kaug, vt, nb, fox_q_gain.shape[-1], t_attn, heads_per_step=4)
        x = _out_mlp(x, ya, yb, w_out[layer], norm_mlp[layer], w_mlp_in[layer], w_mlp_out[layer], ts,
                     ya_transposed=layer % 2 == 1)
    return x
```

```python
import functools
import math

import jax
import jax.numpy as jnp
from jax import lax
from jax.experimental import pallas as pl
from jax.experimental.pallas import tpu as pltpu

EPS = 1e-6
NEG_INF = -1e30
LRU_C = 8.0
CONV_WIDTH = 4
LOG2E = math.log2(math.e)
SUBLANES = 8
LANES = 128
BF16_ROWS = 16
MXU_TILE = 256
VMEM_LIMIT_BYTES = 56 * 1024 * 1024

F32 = jnp.float32
BF16 = jnp.bfloat16


def _rms(x):
    return x * lax.rsqrt(jnp.mean(x * x, axis=-1, keepdims=True) + EPS)


def _dot(a, b):
    return jnp.dot(a, b, preferred_element_type=F32)


def _dot_nt(a, b):
    return lax.dot_general(a, b, (((1,), (1,)), ((), ())), preferred_element_type=F32)


def _softplus(x):
    return jnp.maximum(x, 0.0) + jnp.log(1.0 + jnp.exp(-jnp.abs(x)))


def _gelu_tanh(x):
    c = math.sqrt(2.0 / math.pi)
    return x * (0.5 * (1.0 + jnp.tanh(c * (x + 0.044715 * (x * x * x)))))


def _const_spec(shape):
    nd = len(shape)
    return pl.BlockSpec(shape, lambda *_: (0,) * nd)


def _mem_kv_kernel(mem_ref, g_ref, w_ref, kg_ref, k_ref, v_ref, *, heads, hd):
    width = heads * hd
    hn = (_rms(mem_ref[0]) * g_ref[0]).astype(BF16)
    kv = _dot(hn, w_ref[0])
    for h in range(heads):
        kh = kv[:, h * hd:(h + 1) * hd]
        k_ref[0, 0, :, h * hd:(h + 1) * hd] = (_rms(kh) * kg_ref[0]).astype(BF16)
    v_ref[0, 0] = kv[:, width:].astype(BF16)


def _mem_kv(mem, norm_mem, w_mem_kv, xa_k_gain):
    depth, d_model, two_w = w_mem_kv.shape
    b, m, _ = mem.shape
    width = two_w // 2
    hd = xa_k_gain.shape[-1]
    heads = width // hd
    out = jax.ShapeDtypeStruct((depth, b, m, width), BF16)
    return pl.pallas_call(
        functools.partial(_mem_kv_kernel, heads=heads, hd=hd),
        grid=(depth, b),
        in_specs=[
            pl.BlockSpec((1, m, d_model), lambda l, i: (i, 0, 0)),
            pl.BlockSpec((1, 1, d_model), lambda l, i: (l, 0, 0)),
            pl.BlockSpec((1, d_model, two_w), lambda l, i: (l, 0, 0)),
            pl.BlockSpec((1, 1, hd), lambda l, i: (l, 0, 0)),
        ],
        out_specs=[
            pl.BlockSpec((1, 1, m, width), lambda l, i: (l, i, 0, 0)),
            pl.BlockSpec((1, 1, m, width), lambda l, i: (l, i, 0, 0)),
        ],
        out_shape=[out, out],
        compiler_params=pltpu.CompilerParams(
            dimension_semantics=("arbitrary", "arbitrary"),
            vmem_limit_bytes=VMEM_LIMIT_BYTES),
        name="mem_kv",
    )(mem, norm_mem.reshape(depth, 1, d_model), w_mem_kv.astype(BF16),
      xa_k_gain.reshape(depth, 1, hd))


def _mem_attn(xq, k_ref, v_ref, qg, heads, hd):
    scale = LOG2E / math.sqrt(hd)
    outs = []
    for h in range(heads):
        q = (_rms(xq[:, h * hd:(h + 1) * hd]) * (qg * scale)).astype(BF16)
        s = _dot_nt(q, k_ref[0, :, h * hd:(h + 1) * hd])
        p = jnp.exp2(s - jnp.max(s, axis=-1, keepdims=True))
        l = jnp.sum(p, axis=-1, keepdims=True)
        o = _dot(p.astype(BF16), v_ref[0, :, h * hd:(h + 1) * hd])
        outs.append(o / l)
    return jnp.concatenate(outs, axis=1)


def _lru_front_kernel(x_ref, g_ref, w_in_ref, cw_ref, cb_ref, wg_ref, br_ref, bi_ref,
                      lam_ref, k_ref, v_ref, qg_ref, y_ref, ym_ref,
                      u_tail, a_s, b_s, hc_ref, *, ts, pitch, d_lru, xa_heads, xa_hd):
    half = d_lru // 2

    @pl.when(pl.program_id(1) == 0)
    def _():
        u_tail[...] = jnp.zeros_like(u_tail)
        hc_ref[...] = jnp.zeros_like(hc_ref)

    h = (_rms(x_ref[0]) * g_ref[...]).astype(BF16)
    proj = _dot(h, w_in_ref[...])
    u = proj[:, :d_lru]
    gate = proj[:, d_lru:2 * d_lru]
    xq = proj[:, 2 * d_lru:]

    tail = u_tail[...]
    row8 = lax.broadcasted_iota(jnp.int32, (SUBLANES, d_lru), 0)
    uc = cb_ref[...] + cw_ref[CONV_WIDTH - 1:CONV_WIDTH, :] * u
    for d in range(1, CONV_WIDTH):
        rolled = pltpu.roll(u, d, 0)
        top = jnp.where(row8 < d, pltpu.roll(tail, d, 0), rolled[:SUBLANES])
        shifted = jnp.concatenate([top, rolled[SUBLANES:]], axis=0)
        uc = uc + cw_ref[CONV_WIDTH - 1 - d:CONV_WIDTH - d, :] * shifted
    u_tail[...] = u[ts - SUBLANES:, :]

    ub = uc.astype(BF16)
    g0 = _dot(ub[:, :half], wg_ref[0])
    g1 = _dot(ub[:, half:], wg_ref[1])
    r_pre = jnp.concatenate([g0[:, :half], g1[:, :half]], axis=1)
    i_pre = jnp.concatenate([g0[:, half:], g1[:, half:]], axis=1)
    n = (-0.5 * LRU_C * LOG2E) * _softplus(-lam_ref[...])
    a = jnp.exp2(n * jnp.tanh(r_pre + br_ref[...]) + n)
    i = 0.5 * jnp.tanh(i_pre + bi_ref[...]) + 0.5
    b = jnp.exp2((0.5 * LOG2E) * jnp.log(1.0 - a * a)) * (i * uc)

    seg = ts // SUBLANES
    nslab = d_lru // LANES
    for c in range(nslab):
        for sg in range(SUBLANES):
            a_s[c, pitch * sg:pitch * sg + seg, :] = a[seg * sg:seg * (sg + 1), LANES * c:LANES * (c + 1)]
            b_s[c, pitch * sg:pitch * sg + seg, :] = b[seg * sg:seg * (sg + 1), LANES * c:LANES * (c + 1)]

    def step(k, carry):
        out = []
        for c in range(nslab):
            rows = pl.ds(k, SUBLANES, stride=pitch)
            hv, pv = carry[c]
            av = a_s[c, rows, :]
            hv = av * hv + b_s[c, rows, :]
            pv = av * pv
            a_s[c, rows, :] = pv
            b_s[c, rows, :] = hv
            out.append((hv, pv))
        return tuple(out)

    zero = jnp.zeros((SUBLANES, LANES), F32)
    ends = lax.fori_loop(0, seg, step, ((zero, zero + 1.0),) * nslab, unroll=True)

    row8 = lax.broadcasted_iota(jnp.int32, (SUBLANES, LANES), 0)
    gl = _gelu_tanh(gate)
    for c in range(nslab):
        bv, av = ends[c]
        for d in (1, 2, 4):
            keep = row8 >= d
            a_sh = jnp.where(keep, pltpu.roll(av, d, 0), 1.0)
            b_sh = jnp.where(keep, pltpu.roll(bv, d, 0), 0.0)
            bv = av * b_sh + bv
            av = av * a_sh
        hc = hc_ref[:, LANES * c:LANES * (c + 1)]
        end_state = bv + av * hc
        enter = jnp.where(row8 == 0, hc, pltpu.roll(end_state, 1, 0))
        hc_ref[:, LANES * c:LANES * (c + 1)] = end_state[SUBLANES - 1:SUBLANES, :]
        for sg in range(SUBLANES):
            rows = slice(pitch * sg, pitch * sg + seg)
            hs = b_s[c, rows, :] + a_s[c, rows, :] * enter[sg:sg + 1, :]
            y_ref[0, seg * sg:seg * (sg + 1), LANES * c:LANES * (c + 1)] = (
                hs * gl[seg * sg:seg * (sg + 1), LANES * c:LANES * (c + 1)]).astype(BF16)
    ym_ref[0] = _mem_attn(xq, k_ref, v_ref, qg_ref[...], xa_heads, xa_hd).astype(BF16)


def _lru_front(x, norm_g, w_in, conv_w, conv_b, w_r, b_r, w_i, b_i, lam, k_mem, v_mem, xa_qg, ts):
    b, s, d_model = x.shape
    d_lru = conv_w.shape[-1]
    half = d_lru // 2
    xa_hd = xa_qg.shape[-1]
    xa_width = w_in.shape[1] - 2 * d_lru
    xa_heads = xa_width // xa_hd
    m = k_mem.shape[1]
    blk = w_r.shape[-1]
    nblk_half = half // blk
    wg = (0.5 * jnp.stack([
        jnp.concatenate([jax.scipy.linalg.block_diag(*w_r[hf * nblk_half:(hf + 1) * nblk_half]),
                         jax.scipy.linalg.block_diag(*w_i[hf * nblk_half:(hf + 1) * nblk_half])], axis=1)
        for hf in range(2)])).astype(BF16)
    b_r = 0.5 * b_r
    b_i = 0.5 * b_i
    seg = ts // SUBLANES
    pitch = seg + SUBLANES if (seg // SUBLANES) % 2 == 0 else seg + 2 * SUBLANES
    row = lambda v: v.reshape(1, -1)
    out = jax.ShapeDtypeStruct((b, s, d_lru), BF16)
    outm = jax.ShapeDtypeStruct((b, s, xa_width), BF16)
    return pl.pallas_call(
        functools.partial(_lru_front_kernel, ts=ts, pitch=pitch, d_lru=d_lru, xa_heads=xa_heads,
                          xa_hd=xa_hd),
        grid=(b, s // ts),
        in_specs=[
            pl.BlockSpec((1, ts, d_model), lambda i, j: (i, j, 0)),
            _const_spec((1, d_model)),
            _const_spec(w_in.shape),
            _const_spec(conv_w.shape),
            _const_spec((1, d_lru)),
            _const_spec(wg.shape),
            _const_spec((1, d_lru)),
            _const_spec((1, d_lru)),
            _const_spec((1, d_lru)),
            pl.BlockSpec((1, m, xa_width), lambda i, j: (i, 0, 0)),
            pl.BlockSpec((1, m, xa_width), lambda i, j: (i, 0, 0)),
            _const_spec((1, xa_hd)),
        ],
        out_specs=[
            pl.BlockSpec((1, ts, d_lru), lambda i, j: (i, j, 0)),
            pl.BlockSpec((1, ts, xa_width), lambda i, j: (i, j, 0)),
        ],
        out_shape=[out, outm],
        scratch_shapes=[
            pltpu.VMEM((SUBLANES, d_lru), F32),
            pltpu.VMEM((d_lru // LANES, SUBLANES * pitch, LANES), F32),
            pltpu.VMEM((d_lru // LANES, SUBLANES * pitch, LANES), F32),
            pltpu.VMEM((1, d_lru), F32),
        ],
        compiler_params=pltpu.CompilerParams(
            dimension_semantics=("arbitrary", "arbitrary"),
            vmem_limit_bytes=VMEM_LIMIT_BYTES),
        name="lru_front",
    )(x, row(norm_g), w_in.astype(BF16), conv_w, row(conv_b), wg, row(b_r), row(b_i), row(lam),
      k_mem, v_mem, row(xa_qg))


def _fox_front_kernel(x_ref, g_ref, w_ref, hm_ref, qg_ref, kg_ref, bf_ref, pc_ref, k_ref, v_ref,
                      xqg_ref, qt_out, ka_out, vt_out, nb_out, ym_ref, c_carry,
                      *, ts, sub, width, heads, xa_heads, xa_hd):
    @pl.when(pl.program_id(1) == 0)
    def _():
        c_carry[...] = jnp.zeros_like(c_carry)

    xa_width = xa_heads * xa_hd
    col_blocks = [(0, width), (width, 2 * width), (2 * width, 3 * width),
                  (3 * width, 3 * width + xa_width), (3 * width + xa_width, w_ref.shape[1])]

    def project(r0, out):
        h = (_rms(x_ref[0, r0:r0 + sub, :]) * g_ref[...]).astype(BF16)
        for c0, c1 in col_blocks:
            out.append(_dot(h, w_ref[:, c0:c1]))
            yield

    def head_ms(t):
        t2 = (t * t).astype(BF16)
        return jnp.concatenate(
            [_dot(t2[:, c:c + MXU_TILE], hm_ref[...]) for c in range(0, width, MXU_TILE)], axis=1)

    lane = lax.broadcasted_iota(jnp.int32, (1, LANES), 1)
    one_lanes = (lane >= 4) & (lane < 7)
    row = lax.broadcasted_iota(jnp.int32, (sub, LANES), 0)

    def finish(r0, proj):
        q, k, v, xq, f = proj
        qt_out[0, :, r0:r0 + sub] = jnp.transpose(
            q * lax.rsqrt(head_ms(q) + EPS) * qg_ref[...]).astype(BF16)
        yield
        vt_out[0, :, r0:r0 + sub] = jnp.transpose(v).astype(BF16)
        kn = (k * lax.rsqrt(head_ms(k) + EPS) * kg_ref[...]).astype(BF16)
        yield
        c = -_softplus(-(f + bf_ref[...]))
        d = 1
        while d < sub:
            c = c + jnp.where(row >= d, pltpu.roll(c, d, 0), 0.0)
            d *= 2
        c = c + c_carry[...]
        c_carry[...] = c[sub - 1:sub, :]
        b2 = c * (-LOG2E)
        hi = b2.astype(BF16).astype(F32)
        r1 = b2 - hi
        mid = r1.astype(BF16).astype(F32)
        piece = pc_ref[...]
        pieces = jnp.where(piece == 0, hi, jnp.where(piece == 1, mid, r1 - mid))
        nb_out[0, :, r0:r0 + sub] = -jnp.transpose(pieces)[:SUBLANES * heads, :]
        for p in range(width // LANES):
            bias = pieces if p == 0 else pltpu.roll(pieces, LANES - 2 * SUBLANES * p, 1)
            ka_out[0, r0:r0 + sub, 2 * LANES * p:2 * LANES * p + LANES] = kn[:, LANES * p:LANES * (p + 1)]
            ka_out[0, r0:r0 + sub, 2 * LANES * p + LANES:2 * LANES * (p + 1)] = (
                jnp.where(one_lanes, 1.0, bias).astype(BF16))
        yield
        ym_ref[0, r0:r0 + sub, :] = _mem_attn(xq, k_ref, v_ref, xqg_ref[...], xa_heads, xa_hd).astype(BF16)
        yield

    proj = []
    for _ in project(0, proj):
        pass
    for r0 in range(0, ts, sub):
        nxt = []
        ahead = project(r0 + sub, nxt) if r0 + sub < ts else iter(())
        for _ in finish(r0, proj):
            next(ahead, None)
        for _ in ahead:
            pass
        proj = nxt


def _fox_front(x, norm_g, w_in, b_f, q_gain, k_gain, k_mem, v_mem, xa_qg, ts):
    b, s, d_model = x.shape
    hd = q_gain.shape[-1]
    heads = b_f.shape[-1]
    width = heads * hd
    pairs = width // LANES
    xa_hd = xa_qg.shape[-1]
    xa_width = w_in.shape[1] - 3 * width - heads
    xa_heads = xa_width // xa_hd
    m = k_mem.shape[1]
    w_f = jnp.repeat(w_in[:, 3 * width:3 * width + heads], SUBLANES, axis=1)
    w_all = jnp.concatenate(
        [w_in[:, :3 * width], w_in[:, 3 * width + heads:], w_f,
         jnp.zeros((d_model, LANES - SUBLANES * heads), w_in.dtype)], axis=1).astype(BF16)
    hid = jnp.arange(MXU_TILE) // hd
    head_mean = jnp.where(hid[:, None] == hid[None, :], 1.0 / hd, 0.0).astype(BF16)
    piece = (jnp.arange(LANES, dtype=jnp.int32) % SUBLANES).reshape(1, LANES)
    qg = jnp.tile(q_gain * (LOG2E / math.sqrt(hd)), heads).reshape(1, width)
    kg = jnp.tile(k_gain, heads).reshape(1, width)
    bfp = jnp.zeros((1, LANES), F32).at[0, :SUBLANES * heads].set(jnp.repeat(b_f, SUBLANES))
    row = lambda t: t.reshape(1, -1)
    act_t = jax.ShapeDtypeStruct((b, width, s), BF16)
    return pl.pallas_call(
        functools.partial(_fox_front_kernel, ts=ts, sub=min(ts, MXU_TILE), width=width, heads=heads,
                          xa_heads=xa_heads, xa_hd=xa_hd),
        grid=(b, s // ts),
        in_specs=[
            pl.BlockSpec((1, ts, d_model), lambda i, j: (i, j, 0)),
            _const_spec((1, d_model)),
            _const_spec(w_all.shape),
            _const_spec(head_mean.shape),
            _const_spec((1, width)),
            _const_spec((1, width)),
            _const_spec((1, LANES)),
            _const_spec((1, LANES)),
            pl.BlockSpec((1, m, xa_width), lambda i, j: (i, 0, 0)),
            pl.BlockSpec((1, m, xa_width), lambda i, j: (i, 0, 0)),
            _const_spec((1, xa_hd)),
        ],
        out_specs=[
            pl.BlockSpec((1, width, ts), lambda i, j: (i, 0, j)),
            pl.BlockSpec((1, ts, 2 * width), lambda i, j: (i, j, 0)),
            pl.BlockSpec((1, width, ts), lambda i, j: (i, 0, j)),
            pl.BlockSpec((1, SUBLANES * heads, ts), lambda i, j: (i, 0, j)),
            pl.BlockSpec((1, ts, xa_width), lambda i, j: (i, j, 0)),
        ],
        out_shape=[act_t, jax.ShapeDtypeStruct((b, s, 2 * width), BF16), act_t,
                   jax.ShapeDtypeStruct((b, SUBLANES * heads, s), F32),
                   jax.ShapeDtypeStruct((b, s, xa_width), BF16)],
        scratch_shapes=[pltpu.VMEM((1, LANES), F32)],
        compiler_params=pltpu.CompilerParams(
            dimension_semantics=("arbitrary", "arbitrary"),
            vmem_limit_bytes=VMEM_LIMIT_BYTES),
        name="fox_front",
    )(x, row(norm_g), w_all, head_mean, qg, kg, bfp, piece, k_mem, v_mem, row(xa_qg))


def _fox_attn_kernel(qt_ref, qtn_ref, ka_ref, vt_ref, nb_ref, nbn_ref, o_ref,
                     s0_ref, s1_ref, sf_ref, m_ref, acc_ref, *, t, hd, heads, parts):
    i = pl.program_id(2)
    rowq = lax.broadcasted_iota(jnp.int32, (2 * hd, t), 0)
    row8 = lax.broadcasted_iota(jnp.int32, (SUBLANES, t), 0)
    keys = lax.broadcasted_iota(jnp.int32, (t, t), 0)
    queries = lax.broadcasted_iota(jnp.int32, (t, t), 1)
    tp = t // parts

    def weights(h, q_ref, b_ref):
        slot = h % 2
        qt = q_ref[0, 2 * hd * (h // 2):2 * hd * (h // 2 + 1), :]
        top = jnp.where((rowq >= hd * slot) & (rowq < hd * (slot + 1)), qt, jnp.zeros_like(qt))
        nb = pltpu.roll(b_ref[0, SUBLANES * h:SUBLANES * (h + 1), :], 4, 0)
        ones0 = jnp.where(row8 < 3, 1.0 if slot == 0 else 0.0, 0.0)
        ones1 = jnp.where(row8 < 3, 1.0 if slot == 1 else 0.0, 0.0)
        g0 = jnp.where((row8 >= 4) & (row8 < 7), nb, ones0)
        bot = jnp.concatenate([g0, ones1, jnp.zeros((2 * hd - 2 * SUBLANES, t), F32)], axis=0)
        return jnp.concatenate([top, bot.astype(BF16)], axis=0)

    w_here = [weights(h, qt_ref, nb_ref) for h in range(heads)]
    w_next = [weights(h, qtn_ref, nbn_ref) for h in range(heads)]

    def logits(j, s_ref, w, h, part):
        rows = pl.ds(pl.multiple_of(j * t, t) + tp * part, tp)
        ka = ka_ref[0, rows, 4 * hd * (h // 2):4 * hd * (h // 2 + 1)]
        s_ref[h, tp * part:tp * (part + 1), :] = _dot(ka, w[h]).astype(BF16)

    def block(jc, s_cur, masked, nxt):
        jn, s_next, w = nxt
        for h in range(heads):
            st = s_cur[h]
            if masked:
                st = jnp.where(keys <= queries, st, jnp.asarray(NEG_INF, BF16))
            m = m_ref[h]
            mn = jnp.maximum(m, jnp.max(st, axis=0, keepdims=True).astype(F32))
            m_ref[h] = mn
            mb = mn.astype(BF16)
            pv = None
            for part in range(parts):
                logits(jn, s_next, w, h, part)
                pt = jnp.exp2(st[tp * part:tp * (part + 1), :] - mb)
                cols = pl.ds(pl.multiple_of(jc * t, t) + tp * part, tp)
                vt = jnp.concatenate([vt_ref[0, hd * h:hd * (h + 1), cols], ones_row], axis=0)
                d = _dot(vt, pt)
                pv = d if pv is None else pv + d
            acc_ref[h] = jnp.exp2(m - mn) * acc_ref[h] + pv

    m_ref[...] = jnp.full(m_ref.shape, NEG_INF, F32)
    acc_ref[...] = jnp.zeros_like(acc_ref)
    ones_row = (lax.broadcasted_iota(jnp.int32, (BF16_ROWS, tp), 0) == 0).astype(BF16)

    first_of_next = (0, sf_ref, w_next)

    @pl.when(i == 0)
    def _():
        for h in range(heads):
            for part in range(parts):
                logits(0, sf_ref, w_here, h, part)
        block(0, sf_ref, True, first_of_next)

    @pl.when(i > 0)
    def _():
        block(0, sf_ref, False, (1, s1_ref, w_here))

    def two_blocks(jj, carry):
        j = 2 * jj + 1
        block(j, s1_ref, False, (j + 1, s0_ref, w_here))
        block(j + 1, s0_ref, False, (j + 2, s1_ref, w_here))
        return carry

    lax.fori_loop(0, lax.shift_right_logical(jnp.maximum(i - 1, 0), 1), two_blocks, 0)

    @pl.when(lax.bitwise_and(i, 1) == 1)
    def _():
        block(i, s1_ref, True, first_of_next)

    @pl.when((lax.bitwise_and(i, 1) == 0) & (i > 0))
    def _():
        block(i - 1, s1_ref, False, (i, s0_ref, w_here))
        block(i, s0_ref, True, first_of_next)

    for h in range(heads):
        o_ref[0, hd * h:hd * (h + 1), :] = (acc_ref[h, :hd, :] / acc_ref[h, hd:hd + 1, :]).astype(BF16)


def _fox_attn(qt, kaug, vt, nb, hd, t, heads_per_step):
    b, width, s = qt.shape
    hps = heads_per_step
    nq = s // t
    here = lambda bi, p, i: (bi, p, i)
    nxt = lambda bi, p, i: (bi, p, jnp.minimum(i + 1, nq - 1))
    return pl.pallas_call(
        functools.partial(_fox_attn_kernel, t=t, hd=hd, heads=hps, parts=2),
        grid=(b, width // (hps * hd), nq),
        in_specs=[
            pl.BlockSpec((1, hps * hd, t), here),
            pl.BlockSpec((1, hps * hd, t), nxt),
            pl.BlockSpec((1, s, 2 * hps * hd), lambda bi, p, i: (bi, 0, p)),
            pl.BlockSpec((1, hps * hd, s), lambda bi, p, i: (bi, p, 0)),
            pl.BlockSpec((1, hps * SUBLANES, t), here),
            pl.BlockSpec((1, hps * SUBLANES, t), nxt),
        ],
        out_specs=pl.BlockSpec((1, hps * hd, t), here),
        out_shape=jax.ShapeDtypeStruct((b, width, s), BF16),
        scratch_shapes=[
            pltpu.VMEM((hps, t, t), BF16),
            pltpu.VMEM((hps, t, t), BF16),
            pltpu.VMEM((hps, t, t), BF16),
            pltpu.VMEM((hps, 1, t), F32),
            pltpu.VMEM((hps, hd + BF16_ROWS, t), F32),
        ],
        compiler_params=pltpu.CompilerParams(
            dimension_semantics=("arbitrary", "arbitrary", "arbitrary"),
            vmem_limit_bytes=VMEM_LIMIT_BYTES),
        name="fox_attn",
    )(qt, qt, kaug, vt, nb, nb)


def _out_mlp_kernel(x_ref, ya_ref, yb_ref, wo_ref, g_ref, w1_ref, w2_ref, o_ref,
                    *, wa, ff_chunk, ya_transposed):
    if ya_transposed:
        mix = lax.dot_general(ya_ref[0], wo_ref[0, :wa, :], (((0,), (0,)), ((), ())),
                              preferred_element_type=F32)
    else:
        mix = _dot(ya_ref[0], wo_ref[0, :wa, :])
    x1 = x_ref[0] + mix + _dot(yb_ref[0], wo_ref[0, wa:, :])
    h = (_rms(x1) * g_ref[0]).astype(BF16)
    d_ff = w1_ref.shape[2]
    o_ref[0] = x1
    for c0 in range(0, d_ff, ff_chunk):
        a = jnp.maximum(_dot(h, w1_ref[0, :, c0:c0 + ff_chunk]), 0.0)
        o_ref[0] += _dot((a * a).astype(BF16), w2_ref[0, c0:c0 + ff_chunk, :])


def _out_mlp(x, ya, yb, w_out, norm_g, w1, w2, layer, tm, ya_transposed):
    b, s, d_model = x.shape
    wb = yb.shape[-1]
    wa = w_out.shape[1] - wb
    d_ff = w1.shape[2]
    of_layer = lambda shape: pl.BlockSpec((1,) + shape[1:], lambda i, j: (layer,) + (0,) * (len(shape) - 1))
    if ya_transposed:
        ya_spec = pl.BlockSpec((1, wa, tm), lambda i, j: (i, 0, j))
    else:
        ya_spec = pl.BlockSpec((1, tm, wa), lambda i, j: (i, j, 0))
    return pl.pallas_call(
        functools.partial(_out_mlp_kernel, wa=wa, ff_chunk=min(d_ff, 1024), ya_transposed=ya_transposed),
        grid=(b, s // tm),
        in_specs=[
            pl.BlockSpec((1, tm, d_model), lambda i, j: (i, j, 0)),
            ya_spec,
            pl.BlockSpec((1, tm, wb), lambda i, j: (i, j, 0)),
            of_layer(w_out.shape),
            of_layer(norm_g.shape),
            of_layer(w1.shape),
            of_layer(w2.shape),
        ],
        out_specs=pl.BlockSpec((1, tm, d_model), lambda i, j: (i, j, 0)),
        out_shape=jax.ShapeDtypeStruct((b, s, d_model), F32),
        compiler_params=pltpu.CompilerParams(
            dimension_semantics=("arbitrary", "arbitrary"),
            vmem_limit_bytes=VMEM_LIMIT_BYTES),
        name="out_mlp",
    )(x, ya, yb, w_out, norm_g, w1, w2)


def kernel(x, mem, norm_mix, norm_mem, w_mem_kv, xa_q_gain, xa_k_gain, w_out, norm_mlp, w_mlp_in, w_mlp_out, w_in_a, conv_w, conv_b, w_rgate, b_rgate, w_igate, b_igate, lru_lambda, w_in_b, b_forget, fox_q_gain, fox_k_gain):
    depth = norm_mix.shape[0]
    s = x.shape[1]
    ts = min(s, 512)
    ts_front = min(s, 1024)
    t_attn = min(s, 512)
    k_mem, v_mem = _mem_kv(mem, norm_mem, w_mem_kv, xa_k_gain)
    w_out_b, w1_b, w2_b = w_out.astype(BF16), w_mlp_in.astype(BF16), w_mlp_out.astype(BF16)
    g_mlp = norm_mlp.reshape(depth, 1, -1)
    for layer in range(depth):
        j = layer // 2
        if layer % 2 == 0:
            ya, yb = _lru_front(x, norm_mix[layer], w_in_a[j], conv_w[j], conv_b[j], w_rgate[j], b_rgate[j],
                                w_igate[j], b_igate[j], lru_lambda[j], k_mem[layer], v_mem[layer],
                                xa_q_gain[layer], ts_front)
        else:
            qt, kaug, vt, nb, yb = _fox_front(x, norm_mix[layer], w_in_b[j], b_forget[j], fox_q_gain[j],
                                              fox_k_gain[j], k_mem[layer], v_mem[layer], xa_q_gain[layer],
                                              ts_front)
            ya = _fox_attn(qt, kaug, vt, nb, fox_q_gain.shape[-1], t_attn, heads_per_step=4)
        x = _out_mlp(x, ya, yb, w_out_b, g_mlp, w1_b, w2_b, layer, ts_front, ya_transposed=layer % 2 == 1)
    return x
```

```python
import functools
import math

import jax
import jax.numpy as jnp
from jax import lax
from jax.experimental import pallas as pl
from jax.experimental.pallas import tpu as pltpu

EPS = 1e-6
NEG_INF = -1e30
LRU_C = 8.0
CONV_WIDTH = 4
LOG2E = math.log2(math.e)
SUBLANES = 8
LANES = 128
BF16_ROWS = 16
MXU_TILE = 256
VMEM_LIMIT_BYTES = 56 * 1024 * 1024
ROW_TILE = 4 * MXU_TILE
ATTN_TILE = 2 * MXU_TILE
ATTN_HEADS_PER_STEP = 4

F32 = jnp.float32
BF16 = jnp.bfloat16


def _rms(x):
    return x * lax.rsqrt(jnp.mean(x * x, axis=-1, keepdims=True) + EPS)


def _dot(a, b):
    return jnp.dot(a, b, preferred_element_type=F32)


def _dot_nt(a, b):
    return lax.dot_general(a, b, (((1,), (1,)), ((), ())), preferred_element_type=F32)


def _softplus(x):
    return jnp.maximum(x, 0.0) + jnp.log(1.0 + jnp.exp(-jnp.abs(x)))


def _gelu_tanh(x):
    c = math.sqrt(2.0 / math.pi)
    return x * (0.5 * (1.0 + jnp.tanh(c * (x + 0.044715 * (x * x * x)))))


def _const_spec(shape):
    nd = len(shape)
    return pl.BlockSpec(shape, lambda *_: (0,) * nd)


def _mem_kv_kernel(mem_ref, g_ref, w_ref, kg_ref, k_ref, v_ref, *, heads, hd):
    width = heads * hd
    hn = (_rms(mem_ref[0]) * g_ref[0]).astype(BF16)
    kv = _dot(hn, w_ref[0])
    for h in range(heads):
        kh = kv[:, h * hd:(h + 1) * hd]
        k_ref[0, 0, :, h * hd:(h + 1) * hd] = (_rms(kh) * kg_ref[0]).astype(BF16)
    v_ref[0, 0] = kv[:, width:].astype(BF16)


def _mem_kv(mem, norm_mem, w_mem_kv, xa_k_gain):
    depth, d_model, two_w = w_mem_kv.shape
    b, m, _ = mem.shape
    width = two_w // 2
    hd = xa_k_gain.shape[-1]
    heads = width // hd
    out = jax.ShapeDtypeStruct((depth, b, m, width), BF16)
    return pl.pallas_call(
        functools.partial(_mem_kv_kernel, heads=heads, hd=hd),
        grid=(depth, b),
        in_specs=[
            pl.BlockSpec((1, m, d_model), lambda l, i: (i, 0, 0)),
            pl.BlockSpec((1, 1, d_model), lambda l, i: (l, 0, 0)),
            pl.BlockSpec((1, d_model, two_w), lambda l, i: (l, 0, 0)),
            pl.BlockSpec((1, 1, hd), lambda l, i: (l, 0, 0)),
        ],
        out_specs=[
            pl.BlockSpec((1, 1, m, width), lambda l, i: (l, i, 0, 0)),
            pl.BlockSpec((1, 1, m, width), lambda l, i: (l, i, 0, 0)),
        ],
        out_shape=[out, out],
        compiler_params=pltpu.CompilerParams(
            dimension_semantics=("arbitrary", "arbitrary"),
            vmem_limit_bytes=VMEM_LIMIT_BYTES),
        name="mem_kv",
    )(mem, norm_mem.reshape(depth, 1, d_model), w_mem_kv.astype(BF16),
      xa_k_gain.reshape(depth, 1, hd))


def _mem_attn(xq, k_ref, v_ref, qg, heads, hd):
    scale = LOG2E / math.sqrt(hd)
    outs = []
    for h in range(heads):
        q = (_rms(xq[:, h * hd:(h + 1) * hd]) * (qg * scale)).astype(BF16)
        s = _dot_nt(q, k_ref[0, :, h * hd:(h + 1) * hd])
        p = jnp.exp2(s - jnp.max(s, axis=-1, keepdims=True))
        l = jnp.sum(p, axis=-1, keepdims=True)
        o = _dot(p.astype(BF16), v_ref[0, :, h * hd:(h + 1) * hd])
        outs.append(o / l)
    return jnp.concatenate(outs, axis=1)


def _lru_front_kernel(x_ref, g_ref, w_in_ref, cw_ref, cb_ref, wg_ref, br_ref, bi_ref,
                      lam_ref, k_ref, v_ref, qg_ref, y_ref, ym_ref,
                      u_tail, a_s, b_s, hc_ref, *, ts, pitch, d_lru, xa_heads, xa_hd):
    half = d_lru // 2

    @pl.when(pl.program_id(1) == 0)
    def _():
        u_tail[...] = jnp.zeros_like(u_tail)
        hc_ref[...] = jnp.zeros_like(hc_ref)

    h = (_rms(x_ref[0]) * g_ref[...]).astype(BF16)
    proj = _dot(h, w_in_ref[...])
    u = proj[:, :d_lru]
    gate = proj[:, d_lru:2 * d_lru]
    xq = proj[:, 2 * d_lru:]

    tail = u_tail[...]
    row8 = lax.broadcasted_iota(jnp.int32, (SUBLANES, d_lru), 0)
    uc = cb_ref[...] + cw_ref[CONV_WIDTH - 1:CONV_WIDTH, :] * u
    for d in range(1, CONV_WIDTH):
        rolled = pltpu.roll(u, d, 0)
        top = jnp.where(row8 < d, pltpu.roll(tail, d, 0), rolled[:SUBLANES])
        shifted = jnp.concatenate([top, rolled[SUBLANES:]], axis=0)
        uc = uc + cw_ref[CONV_WIDTH - 1 - d:CONV_WIDTH - d, :] * shifted
    u_tail[...] = u[ts - SUBLANES:, :]

    ub = uc.astype(BF16)
    g0 = _dot(ub[:, :half], wg_ref[0])
    g1 = _dot(ub[:, half:], wg_ref[1])
    r_pre = jnp.concatenate([g0[:, :half], g1[:, :half]], axis=1)
    i_pre = jnp.concatenate([g0[:, half:], g1[:, half:]], axis=1)
    n = (-0.5 * LRU_C * LOG2E) * _softplus(-lam_ref[...])
    a = jnp.exp2(n * jnp.tanh(r_pre + br_ref[...]) + n)
    i = 0.5 * jnp.tanh(i_pre + bi_ref[...]) + 0.5
    b = jnp.exp2((0.5 * LOG2E) * jnp.log(1.0 - a * a)) * (i * uc)

    seg = ts // SUBLANES
    nslab = d_lru // LANES
    for c in range(nslab):
        for sg in range(SUBLANES):
            a_s[c, pitch * sg:pitch * sg + seg, :] = a[seg * sg:seg * (sg + 1), LANES * c:LANES * (c + 1)]
            b_s[c, pitch * sg:pitch * sg + seg, :] = b[seg * sg:seg * (sg + 1), LANES * c:LANES * (c + 1)]

    def step(k, carry):
        out = []
        for c in range(nslab):
            rows = pl.ds(k, SUBLANES, stride=pitch)
            hv, pv = carry[c]
            av = a_s[c, rows, :]
            hv = av * hv + b_s[c, rows, :]
            pv = av * pv
            a_s[c, rows, :] = pv
            b_s[c, rows, :] = hv
            out.append((hv, pv))
        return tuple(out)

    zero = jnp.zeros((SUBLANES, LANES), F32)
    ends = lax.fori_loop(0, seg, step, ((zero, zero + 1.0),) * nslab, unroll=True)

    row8 = lax.broadcasted_iota(jnp.int32, (SUBLANES, LANES), 0)
    gl = _gelu_tanh(gate)
    for c in range(nslab):
        bv, av = ends[c]
        for d in (1, 2, 4):
            keep = row8 >= d
            a_sh = jnp.where(keep, pltpu.roll(av, d, 0), 1.0)
            b_sh = jnp.where(keep, pltpu.roll(bv, d, 0), 0.0)
            bv = av * b_sh + bv
            av = av * a_sh
        hc = hc_ref[:, LANES * c:LANES * (c + 1)]
        end_state = bv + av * hc
        enter = jnp.where(row8 == 0, hc, pltpu.roll(end_state, 1, 0))
        hc_ref[:, LANES * c:LANES * (c + 1)] = end_state[SUBLANES - 1:SUBLANES, :]
        for sg in range(SUBLANES):
            rows = slice(pitch * sg, pitch * sg + seg)
            hs = b_s[c, rows, :] + a_s[c, rows, :] * enter[sg:sg + 1, :]
            y_ref[0, seg * sg:seg * (sg + 1), LANES * c:LANES * (c + 1)] = (
                hs * gl[seg * sg:seg * (sg + 1), LANES * c:LANES * (c + 1)]).astype(BF16)
    ym_ref[0] = _mem_attn(xq, k_ref, v_ref, qg_ref[...], xa_heads, xa_hd).astype(BF16)


def _lru_front(x, norm_g, w_in, conv_w, conv_b, w_r, b_r, w_i, b_i, lam, k_mem, v_mem, xa_qg, ts):
    b, s, d_model = x.shape
    d_lru = conv_w.shape[-1]
    half = d_lru // 2
    xa_hd = xa_qg.shape[-1]
    xa_width = w_in.shape[1] - 2 * d_lru
    xa_heads = xa_width // xa_hd
    m = k_mem.shape[1]
    blk = w_r.shape[-1]
    nblk_half = half // blk
    wg = (0.5 * jnp.stack([
        jnp.concatenate([jax.scipy.linalg.block_diag(*w_r[hf * nblk_half:(hf + 1) * nblk_half]),
                         jax.scipy.linalg.block_diag(*w_i[hf * nblk_half:(hf + 1) * nblk_half])], axis=1)
        for hf in range(2)])).astype(BF16)
    b_r = 0.5 * b_r
    b_i = 0.5 * b_i
    seg = ts // SUBLANES
    pitch = seg + SUBLANES if (seg // SUBLANES) % 2 == 0 else seg + 2 * SUBLANES
    row = lambda v: v.reshape(1, -1)
    out = jax.ShapeDtypeStruct((b, s, d_lru), BF16)
    outm = jax.ShapeDtypeStruct((b, s, xa_width), BF16)
    return pl.pallas_call(
        functools.partial(_lru_front_kernel, ts=ts, pitch=pitch, d_lru=d_lru, xa_heads=xa_heads,
                          xa_hd=xa_hd),
        grid=(b, s // ts),
        in_specs=[
            pl.BlockSpec((1, ts, d_model), lambda i, j: (i, j, 0)),
            _const_spec((1, d_model)),
            _const_spec(w_in.shape),
            _const_spec(conv_w.shape),
            _const_spec((1, d_lru)),
            _const_spec(wg.shape),
            _const_spec((1, d_lru)),
            _const_spec((1, d_lru)),
            _const_spec((1, d_lru)),
            pl.BlockSpec((1, m, xa_width), lambda i, j: (i, 0, 0)),
            pl.BlockSpec((1, m, xa_width), lambda i, j: (i, 0, 0)),
            _const_spec((1, xa_hd)),
        ],
        out_specs=[
            pl.BlockSpec((1, ts, d_lru), lambda i, j: (i, j, 0)),
            pl.BlockSpec((1, ts, xa_width), lambda i, j: (i, j, 0)),
        ],
        out_shape=[out, outm],
        scratch_shapes=[
            pltpu.VMEM((SUBLANES, d_lru), F32),
            pltpu.VMEM((d_lru // LANES, SUBLANES * pitch, LANES), F32),
            pltpu.VMEM((d_lru // LANES, SUBLANES * pitch, LANES), F32),
            pltpu.VMEM((1, d_lru), F32),
        ],
        compiler_params=pltpu.CompilerParams(
            dimension_semantics=("arbitrary", "arbitrary"),
            vmem_limit_bytes=VMEM_LIMIT_BYTES),
        name="lru_front",
    )(x, row(norm_g), w_in.astype(BF16), conv_w, row(conv_b), wg, row(b_r), row(b_i), row(lam),
      k_mem, v_mem, row(xa_qg))


def _fox_front_kernel(x_ref, g_ref, w_ref, hm_ref, qg_ref, kg_ref, bf_ref, pc_ref, k_ref, v_ref,
                      xqg_ref, qt_out, ka_out, vt_out, nb_out, ym_ref, c_carry,
                      *, ts, sub, width, heads, xa_heads, xa_hd):
    @pl.when(pl.program_id(1) == 0)
    def _():
        c_carry[...] = jnp.zeros_like(c_carry)

    xa_width = xa_heads * xa_hd
    col_blocks = [(0, width), (width, 2 * width), (2 * width, 3 * width),
                  (3 * width, 3 * width + xa_width), (3 * width + xa_width, w_ref.shape[1])]

    def project(r0, out):
        h = (_rms(x_ref[0, r0:r0 + sub, :]) * g_ref[...]).astype(BF16)
        for c0, c1 in col_blocks:
            out.append(_dot(h, w_ref[:, c0:c1]))
            yield

    def head_ms(t):
        t2 = (t * t).astype(BF16)
        return jnp.concatenate(
            [_dot(t2[:, c:c + MXU_TILE], hm_ref[...]) for c in range(0, width, MXU_TILE)], axis=1)

    lane = lax.broadcasted_iota(jnp.int32, (1, LANES), 1)
    one_lanes = (lane >= 4) & (lane < 7)
    row = lax.broadcasted_iota(jnp.int32, (sub, LANES), 0)

    def finish(r0, proj):
        q, k, v, xq, f = proj
        qt_out[0, :, r0:r0 + sub] = jnp.transpose(
            q * lax.rsqrt(head_ms(q) + EPS) * qg_ref[...]).astype(BF16)
        yield
        vt_out[0, :, r0:r0 + sub] = jnp.transpose(v).astype(BF16)
        kn = (k * lax.rsqrt(head_ms(k) + EPS) * kg_ref[...]).astype(BF16)
        yield
        c = -_softplus(-(f + bf_ref[...]))
        d = 1
        while d < sub:
            c = c + jnp.where(row >= d, pltpu.roll(c, d, 0), 0.0)
            d *= 2
        c = c + c_carry[...]
        c_carry[...] = c[sub - 1:sub, :]
        b2 = c * (-LOG2E)
        hi = b2.astype(BF16).astype(F32)
        r1 = b2 - hi
        mid = r1.astype(BF16).astype(F32)
        piece = pc_ref[...]
        pieces = jnp.where(piece == 0, hi, jnp.where(piece == 1, mid, r1 - mid))
        nb_out[0, :, r0:r0 + sub] = -jnp.transpose(pieces)[:SUBLANES * heads, :]
        for p in range(width // LANES):
            bias = pieces if p == 0 else pltpu.roll(pieces, LANES - 2 * SUBLANES * p, 1)
            ka_out[0, r0:r0 + sub, 2 * LANES * p:2 * LANES * p + LANES] = kn[:, LANES * p:LANES * (p + 1)]
            ka_out[0, r0:r0 + sub, 2 * LANES * p + LANES:2 * LANES * (p + 1)] = (
                jnp.where(one_lanes, 1.0, bias).astype(BF16))
        yield
        ym_ref[0, r0:r0 + sub, :] = _mem_attn(xq, k_ref, v_ref, xqg_ref[...], xa_heads, xa_hd).astype(BF16)
        yield

    proj = []
    for _ in project(0, proj):
        pass
    for r0 in range(0, ts, sub):
        nxt = []
        ahead = project(r0 + sub, nxt) if r0 + sub < ts else iter(())
        for _ in finish(r0, proj):
            next(ahead, None)
        for _ in ahead:
            pass
        proj = nxt


def _fox_front(x, norm_g, w_in, b_f, q_gain, k_gain, k_mem, v_mem, xa_qg, ts):
    b, s, d_model = x.shape
    hd = q_gain.shape[-1]
    heads = b_f.shape[-1]
    width = heads * hd
    xa_hd = xa_qg.shape[-1]
    xa_width = w_in.shape[1] - 3 * width - heads
    xa_heads = xa_width // xa_hd
    m = k_mem.shape[1]
    w_f = jnp.repeat(w_in[:, 3 * width:3 * width + heads], SUBLANES, axis=1)
    w_all = jnp.concatenate(
        [w_in[:, :3 * width], w_in[:, 3 * width + heads:], w_f,
         jnp.zeros((d_model, LANES - SUBLANES * heads), w_in.dtype)], axis=1).astype(BF16)
    hid = jnp.arange(MXU_TILE) // hd
    head_mean = jnp.where(hid[:, None] == hid[None, :], 1.0 / hd, 0.0).astype(BF16)
    piece = (jnp.arange(LANES, dtype=jnp.int32) % SUBLANES).reshape(1, LANES)
    qg = jnp.tile(q_gain * (LOG2E / math.sqrt(hd)), heads).reshape(1, width)
    kg = jnp.tile(k_gain, heads).reshape(1, width)
    bfp = jnp.zeros((1, LANES), F32).at[0, :SUBLANES * heads].set(jnp.repeat(b_f, SUBLANES))
    row = lambda t: t.reshape(1, -1)
    act_t = jax.ShapeDtypeStruct((b, width, s), BF16)
    return pl.pallas_call(
        functools.partial(_fox_front_kernel, ts=ts, sub=min(ts, MXU_TILE), width=width, heads=heads,
                          xa_heads=xa_heads, xa_hd=xa_hd),
        grid=(b, s // ts),
        in_specs=[
            pl.BlockSpec((1, ts, d_model), lambda i, j: (i, j, 0)),
            _const_spec((1, d_model)),
            _const_spec(w_all.shape),
            _const_spec(head_mean.shape),
            _const_spec((1, width)),
            _const_spec((1, width)),
            _const_spec((1, LANES)),
            _const_spec((1, LANES)),
            pl.BlockSpec((1, m, xa_width), lambda i, j: (i, 0, 0)),
            pl.BlockSpec((1, m, xa_width), lambda i, j: (i, 0, 0)),
            _const_spec((1, xa_hd)),
        ],
        out_specs=[
            pl.BlockSpec((1, width, ts), lambda i, j: (i, 0, j)),
            pl.BlockSpec((1, ts, 2 * width), lambda i, j: (i, j, 0)),
            pl.BlockSpec((1, width, ts), lambda i, j: (i, 0, j)),
            pl.BlockSpec((1, SUBLANES * heads, ts), lambda i, j: (i, 0, j)),
            pl.BlockSpec((1, ts, xa_width), lambda i, j: (i, j, 0)),
        ],
        out_shape=[act_t, jax.ShapeDtypeStruct((b, s, 2 * width), BF16), act_t,
                   jax.ShapeDtypeStruct((b, SUBLANES * heads, s), F32),
                   jax.ShapeDtypeStruct((b, s, xa_width), BF16)],
        scratch_shapes=[pltpu.VMEM((1, LANES), F32)],
        compiler_params=pltpu.CompilerParams(
            dimension_semantics=("arbitrary", "arbitrary"),
            vmem_limit_bytes=VMEM_LIMIT_BYTES),
        name="fox_front",
    )(x, row(norm_g), w_all, head_mean, qg, kg, bfp, piece, k_mem, v_mem, row(xa_qg))


def _fox_attn_kernel(qt_ref, qtn_ref, ka_ref, vt_ref, nb_ref, nbn_ref, o_ref,
                     s0_ref, s1_ref, sf_ref, m_ref, acc_ref, *, t, hd, heads, parts):
    i = pl.program_id(2)
    rowq = lax.broadcasted_iota(jnp.int32, (2 * hd, t), 0)
    row8 = lax.broadcasted_iota(jnp.int32, (SUBLANES, t), 0)
    keys = lax.broadcasted_iota(jnp.int32, (t, t), 0)
    queries = lax.broadcasted_iota(jnp.int32, (t, t), 1)
    tp = t // parts

    def weights(h, q_ref, b_ref):
        slot = h % 2
        qt = q_ref[0, 2 * hd * (h // 2):2 * hd * (h // 2 + 1), :]
        top = jnp.where((rowq >= hd * slot) & (rowq < hd * (slot + 1)), qt, jnp.zeros_like(qt))
        nb = pltpu.roll(b_ref[0, SUBLANES * h:SUBLANES * (h + 1), :], 4, 0)
        ones0 = jnp.where(row8 < 3, 1.0 if slot == 0 else 0.0, 0.0)
        ones1 = jnp.where(row8 < 3, 1.0 if slot == 1 else 0.0, 0.0)
        g0 = jnp.where((row8 >= 4) & (row8 < 7), nb, ones0)
        bot = jnp.concatenate([g0, ones1, jnp.zeros((2 * hd - 2 * SUBLANES, t), F32)], axis=0)
        return jnp.concatenate([top, bot.astype(BF16)], axis=0)

    w_here = [weights(h, qt_ref, nb_ref) for h in range(heads)]
    w_next = [weights(h, qtn_ref, nbn_ref) for h in range(heads)]

    def logits(j, s_ref, w, h, part):
        rows = pl.ds(pl.multiple_of(j * t, t) + tp * part, tp)
        ka = ka_ref[0, rows, 4 * hd * (h // 2):4 * hd * (h // 2 + 1)]
        s_ref[h, tp * part:tp * (part + 1), :] = _dot(ka, w[h]).astype(BF16)

    def block(jc, s_cur, masked, nxt):
        jn, s_next, w = nxt
        for h in range(heads):
            st = s_cur[h]
            if masked:
                st = jnp.where(keys <= queries, st, jnp.asarray(NEG_INF, BF16))
            m = m_ref[h]
            mn = jnp.maximum(m, jnp.max(st, axis=0, keepdims=True).astype(F32))
            m_ref[h] = mn
            mb = mn.astype(BF16)
            pv = None
            for part in range(parts):
                logits(jn, s_next, w, h, part)
                pt = jnp.exp2(st[tp * part:tp * (part + 1), :] - mb)
                cols = pl.ds(pl.multiple_of(jc * t, t) + tp * part, tp)
                vt = jnp.concatenate([vt_ref[0, hd * h:hd * (h + 1), cols], ones_row], axis=0)
                d = _dot(vt, pt)
                pv = d if pv is None else pv + d
            acc_ref[h] = jnp.exp2(m - mn) * acc_ref[h] + pv

    m_ref[...] = jnp.full(m_ref.shape, NEG_INF, F32)
    acc_ref[...] = jnp.zeros_like(acc_ref)
    ones_row = (lax.broadcasted_iota(jnp.int32, (BF16_ROWS, tp), 0) == 0).astype(BF16)

    first_of_next = (0, sf_ref, w_next)

    @pl.when(i == 0)
    def _():
        for h in range(heads):
            for part in range(parts):
                logits(0, sf_ref, w_here, h, part)
        block(0, sf_ref, True, first_of_next)

    @pl.when(i > 0)
    def _():
        block(0, sf_ref, False, (1, s1_ref, w_here))

    def two_blocks(jj, carry):
        j = 2 * jj + 1
        block(j, s1_ref, False, (j + 1, s0_ref, w_here))
        block(j + 1, s0_ref, False, (j + 2, s1_ref, w_here))
        return carry

    lax.fori_loop(0, lax.shift_right_logical(jnp.maximum(i - 1, 0), 1), two_blocks, 0)

    @pl.when(lax.bitwise_and(i, 1) == 1)
    def _():
        block(i, s1_ref, True, first_of_next)

    @pl.when((lax.bitwise_and(i, 1) == 0) & (i > 0))
    def _():
        block(i - 1, s1_ref, False, (i, s0_ref, w_here))
        block(i, s0_ref, True, first_of_next)

    for h in range(heads):
        o_ref[0, hd * h:hd * (h + 1), :] = (acc_ref[h, :hd, :] / acc_ref[h, hd:hd + 1, :]).astype(BF16)


def _fox_attn(qt, kaug, vt, nb, hd, t, heads_per_step):
    b, width, s = qt.shape
    hps = heads_per_step
    nq = s // t
    here = lambda bi, p, i: (bi, p, i)
    nxt = lambda bi, p, i: (bi, p, jnp.minimum(i + 1, nq - 1))
    return pl.pallas_call(
        functools.partial(_fox_attn_kernel, t=t, hd=hd, heads=hps, parts=max(1, t // MXU_TILE)),
        grid=(b, width // (hps * hd), nq),
        in_specs=[
            pl.BlockSpec((1, hps * hd, t), here),
            pl.BlockSpec((1, hps * hd, t), nxt),
            pl.BlockSpec((1, s, 2 * hps * hd), lambda bi, p, i: (bi, 0, p)),
            pl.BlockSpec((1, hps * hd, s), lambda bi, p, i: (bi, p, 0)),
            pl.BlockSpec((1, hps * SUBLANES, t), here),
            pl.BlockSpec((1, hps * SUBLANES, t), nxt),
        ],
        out_specs=pl.BlockSpec((1, hps * hd, t), here),
        out_shape=jax.ShapeDtypeStruct((b, width, s), BF16),
        scratch_shapes=[
            pltpu.VMEM((hps, t, t), BF16),
            pltpu.VMEM((hps, t, t), BF16),
            pltpu.VMEM((hps, t, t), BF16),
            pltpu.VMEM((hps, 1, t), F32),
            pltpu.VMEM((hps, hd + BF16_ROWS, t), F32),
        ],
        compiler_params=pltpu.CompilerParams(
            dimension_semantics=("arbitrary", "arbitrary", "arbitrary"),
            vmem_limit_bytes=VMEM_LIMIT_BYTES),
        name="fox_attn",
    )(qt, qt, kaug, vt, nb, nb)


def _out_mlp_kernel(x_ref, ya_ref, yb_ref, wo_ref, g_ref, w1_ref, w2_ref, o_ref,
                    *, wa, ff_chunk, ya_transposed):
    if ya_transposed:
        mix = lax.dot_general(ya_ref[0], wo_ref[0, :wa, :], (((0,), (0,)), ((), ())),
                              preferred_element_type=F32)
    else:
        mix = _dot(ya_ref[0], wo_ref[0, :wa, :])
    x1 = x_ref[0] + mix + _dot(yb_ref[0], wo_ref[0, wa:, :])
    h = (_rms(x1) * g_ref[0]).astype(BF16)
    d_ff = w1_ref.shape[2]
    o_ref[0] = x1
    for c0 in range(0, d_ff, ff_chunk):
        a = jnp.maximum(_dot(h, w1_ref[0, :, c0:c0 + ff_chunk]), 0.0)
        o_ref[0] += _dot((a * a).astype(BF16), w2_ref[0, c0:c0 + ff_chunk, :])


def _out_mlp(x, ya, yb, w_out, norm_g, w1, w2, layer, tm, ya_transposed):
    b, s, d_model = x.shape
    wb = yb.shape[-1]
    wa = w_out.shape[1] - wb
    d_ff = w1.shape[2]
    of_layer = lambda shape: pl.BlockSpec((1,) + shape[1:], lambda i, j: (layer,) + (0,) * (len(shape) - 1))
    if ya_transposed:
        ya_spec = pl.BlockSpec((1, wa, tm), lambda i, j: (i, 0, j))
    else:
        ya_spec = pl.BlockSpec((1, tm, wa), lambda i, j: (i, j, 0))
    return pl.pallas_call(
        functools.partial(_out_mlp_kernel, wa=wa, ff_chunk=min(d_ff, 1024), ya_transposed=ya_transposed),
        grid=(b, s // tm),
        in_specs=[
            pl.BlockSpec((1, tm, d_model), lambda i, j: (i, j, 0)),
            ya_spec,
            pl.BlockSpec((1, tm, wb), lambda i, j: (i, j, 0)),
            of_layer(w_out.shape),
            of_layer(norm_g.shape),
            of_layer(w1.shape),
            of_layer(w2.shape),
        ],
        out_specs=pl.BlockSpec((1, tm, d_model), lambda i, j: (i, j, 0)),
        out_shape=jax.ShapeDtypeStruct((b, s, d_model), F32),
        compiler_params=pltpu.CompilerParams(
            dimension_semantics=("arbitrary", "arbitrary"),
            vmem_limit_bytes=VMEM_LIMIT_BYTES),
        name="out_mlp",
    )(x, ya, yb, w_out, norm_g, w1, w2)


def kernel(x, mem, norm_mix, norm_mem, w_mem_kv, xa_q_gain, xa_k_gain, w_out, norm_mlp, w_mlp_in, w_mlp_out, w_in_a, conv_w, conv_b, w_rgate, b_rgate, w_igate, b_igate, lru_lambda, w_in_b, b_forget, fox_q_gain, fox_k_gain):
    depth = norm_mix.shape[0]
    s = x.shape[1]
    ts_front = min(s, ROW_TILE)
    t_attn = min(s, ATTN_TILE)
    k_mem, v_mem = _mem_kv(mem, norm_mem, w_mem_kv, xa_k_gain)
    w_out_b, w1_b, w2_b = w_out.astype(BF16), w_mlp_in.astype(BF16), w_mlp_out.astype(BF16)
    g_mlp = norm_mlp.reshape(depth, 1, -1)
    for layer in range(depth):
        j = layer // 2
        if layer % 2 == 0:
            ya, yb = _lru_front(x, norm_mix[layer], w_in_a[j], conv_w[j], conv_b[j], w_rgate[j], b_rgate[j],
                                w_igate[j], b_igate[j], lru_lambda[j], k_mem[layer], v_mem[layer],
                                xa_q_gain[layer], ts_front)
        else:
            qt, kaug, vt, nb, yb = _fox_front(x, norm_mix[layer], w_in_b[j], b_forget[j], fox_q_gain[j],
                                              fox_k_gain[j], k_mem[layer], v_mem[layer], xa_q_gain[layer],
                                              ts_front)
            ya = _fox_attn(qt, kaug, vt, nb, fox_q_gain.shape[-1], t_attn, ATTN_HEADS_PER_STEP)
        x = _out_mlp(x, ya, yb, w_out_b, g_mlp, w1_b, w2_b, layer, ts_front, ya_transposed=layer % 2 == 1)
    return x
```

```python
import functools
import math

import jax
import jax.numpy as jnp
from jax import lax
from jax.experimental import pallas as pl
from jax.experimental.pallas import tpu as pltpu

EPS = 1e-6
NEG_INF = -1e30
LRU_C = 8.0
CONV_WIDTH = 4
LOG2E = math.log2(math.e)
SUBLANES = 8
LANES = 128
BF16_ROWS = 16
MXU_TILE = 256
VMEM_LIMIT_BYTES = 56 * 1024 * 1024
ROW_TILE = 4 * MXU_TILE
ATTN_TILE = MXU_TILE
ATTN_HEADS_PER_STEP = 8

F32 = jnp.float32
BF16 = jnp.bfloat16


def _rms(x):
    return x * lax.rsqrt(jnp.mean(x * x, axis=-1, keepdims=True) + EPS)


def _dot(a, b):
    return jnp.dot(a, b, preferred_element_type=F32)


def _dot_nt(a, b):
    return lax.dot_general(a, b, (((1,), (1,)), ((), ())), preferred_element_type=F32)


def _softplus(x):
    return jnp.maximum(x, 0.0) + jnp.log(1.0 + jnp.exp(-jnp.abs(x)))


def _gelu_tanh(x):
    c = math.sqrt(2.0 / math.pi)
    return x * (0.5 * (1.0 + jnp.tanh(c * (x + 0.044715 * (x * x * x)))))


def _const_spec(shape):
    nd = len(shape)
    return pl.BlockSpec(shape, lambda *_: (0,) * nd)


def _mem_kv_kernel(mem_ref, g_ref, w_ref, kg_ref, k_ref, v_ref, *, heads, hd):
    width = heads * hd
    hn = (_rms(mem_ref[0]) * g_ref[0]).astype(BF16)
    kv = _dot(hn, w_ref[0])
    for h in range(heads):
        kh = kv[:, h * hd:(h + 1) * hd]
        k_ref[0, 0, :, h * hd:(h + 1) * hd] = (_rms(kh) * kg_ref[0]).astype(BF16)
    v_ref[0, 0] = kv[:, width:].astype(BF16)


def _mem_kv(mem, norm_mem, w_mem_kv, xa_k_gain):
    depth, d_model, two_w = w_mem_kv.shape
    b, m, _ = mem.shape
    width = two_w // 2
    hd = xa_k_gain.shape[-1]
    heads = width // hd
    out = jax.ShapeDtypeStruct((depth, b, m, width), BF16)
    return pl.pallas_call(
        functools.partial(_mem_kv_kernel, heads=heads, hd=hd),
        grid=(depth, b),
        in_specs=[
            pl.BlockSpec((1, m, d_model), lambda l, i: (i, 0, 0)),
            pl.BlockSpec((1, 1, d_model), lambda l, i: (l, 0, 0)),
            pl.BlockSpec((1, d_model, two_w), lambda l, i: (l, 0, 0)),
            pl.BlockSpec((1, 1, hd), lambda l, i: (l, 0, 0)),
        ],
        out_specs=[
            pl.BlockSpec((1, 1, m, width), lambda l, i: (l, i, 0, 0)),
            pl.BlockSpec((1, 1, m, width), lambda l, i: (l, i, 0, 0)),
        ],
        out_shape=[out, out],
        compiler_params=pltpu.CompilerParams(
            dimension_semantics=("arbitrary", "arbitrary"),
            vmem_limit_bytes=VMEM_LIMIT_BYTES),
        name="mem_kv",
    )(mem, norm_mem.reshape(depth, 1, d_model), w_mem_kv.astype(BF16),
      xa_k_gain.reshape(depth, 1, hd))


def _mem_attn(xq, k_ref, v_ref, qg, heads, hd):
    scale = LOG2E / math.sqrt(hd)
    outs = []
    for h in range(heads):
        q = (_rms(xq[:, h * hd:(h + 1) * hd]) * (qg * scale)).astype(BF16)
        s = _dot_nt(q, k_ref[0, :, h * hd:(h + 1) * hd])
        p = jnp.exp2(s - jnp.max(s, axis=-1, keepdims=True))
        l = jnp.sum(p, axis=-1, keepdims=True)
        o = _dot(p.astype(BF16), v_ref[0, :, h * hd:(h + 1) * hd])
        outs.append(o / l)
    return jnp.concatenate(outs, axis=1)


def _lru_front_kernel(x_ref, g_ref, w_in_ref, cw_ref, cb_ref, wg_ref, br_ref, bi_ref,
                      lam_ref, k_ref, v_ref, qg_ref, y_ref, ym_ref,
                      u_tail, a_s, b_s, hc_ref, *, ts, pitch, d_lru, xa_heads, xa_hd):
    half = d_lru // 2

    @pl.when(pl.program_id(1) == 0)
    def _():
        u_tail[...] = jnp.zeros_like(u_tail)
        hc_ref[...] = jnp.zeros_like(hc_ref)

    h = (_rms(x_ref[0]) * g_ref[...]).astype(BF16)
    proj = _dot(h, w_in_ref[...])
    u = proj[:, :d_lru]
    gate = proj[:, d_lru:2 * d_lru]
    xq = proj[:, 2 * d_lru:]

    tail = u_tail[...]
    row8 = lax.broadcasted_iota(jnp.int32, (SUBLANES, d_lru), 0)
    uc = cb_ref[...] + cw_ref[CONV_WIDTH - 1:CONV_WIDTH, :] * u
    for d in range(1, CONV_WIDTH):
        rolled = pltpu.roll(u, d, 0)
        top = jnp.where(row8 < d, pltpu.roll(tail, d, 0), rolled[:SUBLANES])
        shifted = jnp.concatenate([top, rolled[SUBLANES:]], axis=0)
        uc = uc + cw_ref[CONV_WIDTH - 1 - d:CONV_WIDTH - d, :] * shifted
    u_tail[...] = u[ts - SUBLANES:, :]

    ub = uc.astype(BF16)
    g0 = _dot(ub[:, :half], wg_ref[0])
    g1 = _dot(ub[:, half:], wg_ref[1])
    r_pre = jnp.concatenate([g0[:, :half], g1[:, :half]], axis=1)
    i_pre = jnp.concatenate([g0[:, half:], g1[:, half:]], axis=1)
    n = (-0.5 * LRU_C * LOG2E) * _softplus(-lam_ref[...])
    a = jnp.exp2(n * jnp.tanh(r_pre + br_ref[...]) + n)
    i = 0.5 * jnp.tanh(i_pre + bi_ref[...]) + 0.5
    b = jnp.exp2((0.5 * LOG2E) * jnp.log(1.0 - a * a)) * (i * uc)

    seg = ts // SUBLANES
    nslab = d_lru // LANES
    for c in range(nslab):
        for sg in range(SUBLANES):
            a_s[c, pitch * sg:pitch * sg + seg, :] = a[seg * sg:seg * (sg + 1), LANES * c:LANES * (c + 1)]
            b_s[c, pitch * sg:pitch * sg + seg, :] = b[seg * sg:seg * (sg + 1), LANES * c:LANES * (c + 1)]

    def step(k, carry):
        out = []
        for c in range(nslab):
            rows = pl.ds(k, SUBLANES, stride=pitch)
            hv, pv = carry[c]
            av = a_s[c, rows, :]
            hv = av * hv + b_s[c, rows, :]
            pv = av * pv
            a_s[c, rows, :] = pv
            b_s[c, rows, :] = hv
            out.append((hv, pv))
        return tuple(out)

    zero = jnp.zeros((SUBLANES, LANES), F32)
    ends = lax.fori_loop(0, seg, step, ((zero, zero + 1.0),) * nslab, unroll=True)

    row8 = lax.broadcasted_iota(jnp.int32, (SUBLANES, LANES), 0)
    gl = _gelu_tanh(gate)
    for c in range(nslab):
        bv, av = ends[c]
        for d in (1, 2, 4):
            keep = row8 >= d
            a_sh = jnp.where(keep, pltpu.roll(av, d, 0), 1.0)
            b_sh = jnp.where(keep, pltpu.roll(bv, d, 0), 0.0)
            bv = av * b_sh + bv
            av = av * a_sh
        hc = hc_ref[:, LANES * c:LANES * (c + 1)]
        end_state = bv + av * hc
        enter = jnp.where(row8 == 0, hc, pltpu.roll(end_state, 1, 0))
        hc_ref[:, LANES * c:LANES * (c + 1)] = end_state[SUBLANES - 1:SUBLANES, :]
        for sg in range(SUBLANES):
            rows = slice(pitch * sg, pitch * sg + seg)
            hs = b_s[c, rows, :] + a_s[c, rows, :] * enter[sg:sg + 1, :]
            y_ref[0, seg * sg:seg * (sg + 1), LANES * c:LANES * (c + 1)] = (
                hs * gl[seg * sg:seg * (sg + 1), LANES * c:LANES * (c + 1)]).astype(BF16)
    ym_ref[0] = _mem_attn(xq, k_ref, v_ref, qg_ref[...], xa_heads, xa_hd).astype(BF16)


def _lru_front(x, norm_g, w_in, conv_w, conv_b, w_r, b_r, w_i, b_i, lam, k_mem, v_mem, xa_qg, ts):
    b, s, d_model = x.shape
    d_lru = conv_w.shape[-1]
    half = d_lru // 2
    xa_hd = xa_qg.shape[-1]
    xa_width = w_in.shape[1] - 2 * d_lru
    xa_heads = xa_width // xa_hd
    m = k_mem.shape[1]
    blk = w_r.shape[-1]
    nblk_half = half // blk
    wg = (0.5 * jnp.stack([
        jnp.concatenate([jax.scipy.linalg.block_diag(*w_r[hf * nblk_half:(hf + 1) * nblk_half]),
                         jax.scipy.linalg.block_diag(*w_i[hf * nblk_half:(hf + 1) * nblk_half])], axis=1)
        for hf in range(2)])).astype(BF16)
    b_r = 0.5 * b_r
    b_i = 0.5 * b_i
    seg = ts // SUBLANES
    pitch = seg + SUBLANES if (seg // SUBLANES) % 2 == 0 else seg + 2 * SUBLANES
    row = lambda v: v.reshape(1, -1)
    out = jax.ShapeDtypeStruct((b, s, d_lru), BF16)
    outm = jax.ShapeDtypeStruct((b, s, xa_width), BF16)
    return pl.pallas_call(
        functools.partial(_lru_front_kernel, ts=ts, pitch=pitch, d_lru=d_lru, xa_heads=xa_heads,
                          xa_hd=xa_hd),
        grid=(b, s // ts),
        in_specs=[
            pl.BlockSpec((1, ts, d_model), lambda i, j: (i, j, 0)),
            _const_spec((1, d_model)),
            _const_spec(w_in.shape),
            _const_spec(conv_w.shape),
            _const_spec((1, d_lru)),
            _const_spec(wg.shape),
            _const_spec((1, d_lru)),
            _const_spec((1, d_lru)),
            _const_spec((1, d_lru)),
            pl.BlockSpec((1, m, xa_width), lambda i, j: (i, 0, 0)),
            pl.BlockSpec((1, m, xa_width), lambda i, j: (i, 0, 0)),
            _const_spec((1, xa_hd)),
        ],
        out_specs=[
            pl.BlockSpec((1, ts, d_lru), lambda i, j: (i, j, 0)),
            pl.BlockSpec((1, ts, xa_width), lambda i, j: (i, j, 0)),
        ],
        out_shape=[out, outm],
        scratch_shapes=[
            pltpu.VMEM((SUBLANES, d_lru), F32),
            pltpu.VMEM((d_lru // LANES, SUBLANES * pitch, LANES), F32),
            pltpu.VMEM((d_lru // LANES, SUBLANES * pitch, LANES), F32),
            pltpu.VMEM((1, d_lru), F32),
        ],
        compiler_params=pltpu.CompilerParams(
            dimension_semantics=("arbitrary", "arbitrary"),
            vmem_limit_bytes=VMEM_LIMIT_BYTES),
        name="lru_front",
    )(x, row(norm_g), w_in.astype(BF16), conv_w, row(conv_b), wg, row(b_r), row(b_i), row(lam),
      k_mem, v_mem, row(xa_qg))


def _fox_front_kernel(x_ref, g_ref, w_ref, hm_ref, qg_ref, kg_ref, bf_ref, pc_ref, k_ref, v_ref,
                      xqg_ref, qt_out, ka_out, vt_out, nb_out, ym_ref, c_carry,
                      *, ts, sub, width, heads, xa_heads, xa_hd):
    @pl.when(pl.program_id(1) == 0)
    def _():
        c_carry[...] = jnp.zeros_like(c_carry)

    xa_width = xa_heads * xa_hd
    col_blocks = [(0, width), (width, 2 * width), (2 * width, 3 * width),
                  (3 * width, 3 * width + xa_width), (3 * width + xa_width, w_ref.shape[1])]

    def project(r0, out):
        h = (_rms(x_ref[0, r0:r0 + sub, :]) * g_ref[...]).astype(BF16)
        for c0, c1 in col_blocks:
            out.append(_dot(h, w_ref[:, c0:c1]))
            yield

    def head_ms(t):
        t2 = (t * t).astype(BF16)
        return jnp.concatenate(
            [_dot(t2[:, c:c + MXU_TILE], hm_ref[...]) for c in range(0, width, MXU_TILE)], axis=1)

    lane = lax.broadcasted_iota(jnp.int32, (1, LANES), 1)
    one_lanes = (lane >= 4) & (lane < 7)
    row = lax.broadcasted_iota(jnp.int32, (sub, LANES), 0)

    def finish(r0, proj):
        q, k, v, xq, f = proj
        qt_out[0, :, r0:r0 + sub] = jnp.transpose(
            q * lax.rsqrt(head_ms(q) + EPS) * qg_ref[...]).astype(BF16)
        yield
        vt_out[0, :, r0:r0 + sub] = jnp.transpose(v).astype(BF16)
        kn = (k * lax.rsqrt(head_ms(k) + EPS) * kg_ref[...]).astype(BF16)
        yield
        c = -_softplus(-(f + bf_ref[...]))
        d = 1
        while d < sub:
            c = c + jnp.where(row >= d, pltpu.roll(c, d, 0), 0.0)
            d *= 2
        c = c + c_carry[...]
        c_carry[...] = c[sub - 1:sub, :]
        b2 = c * (-LOG2E)
        hi = b2.astype(BF16).astype(F32)
        r1 = b2 - hi
        mid = r1.astype(BF16).astype(F32)
        piece = pc_ref[...]
        pieces = jnp.where(piece == 0, hi, jnp.where(piece == 1, mid, r1 - mid))
        nb_out[0, :, r0:r0 + sub] = -jnp.transpose(pieces)[:SUBLANES * heads, :]
        for p in range(width // LANES):
            bias = pieces if p == 0 else pltpu.roll(pieces, LANES - 2 * SUBLANES * p, 1)
            ka_out[0, r0:r0 + sub, 2 * LANES * p:2 * LANES * p + LANES] = kn[:, LANES * p:LANES * (p + 1)]
            ka_out[0, r0:r0 + sub, 2 * LANES * p + LANES:2 * LANES * (p + 1)] = (
                jnp.where(one_lanes, 1.0, bias).astype(BF16))
        yield
        ym_ref[0, r0:r0 + sub, :] = _mem_attn(xq, k_ref, v_ref, xqg_ref[...], xa_heads, xa_hd).astype(BF16)
        yield

    proj = []
    for _ in project(0, proj):
        pass
    for r0 in range(0, ts, sub):
        nxt = []
        ahead = project(r0 + sub, nxt) if r0 + sub < ts else iter(())
        for _ in finish(r0, proj):
            next(ahead, None)
        for _ in ahead:
            pass
        proj = nxt


def _fox_front(x, norm_g, w_in, b_f, q_gain, k_gain, k_mem, v_mem, xa_qg, ts):
    b, s, d_model = x.shape
    hd = q_gain.shape[-1]
    heads = b_f.shape[-1]
    width = heads * hd
    xa_hd = xa_qg.shape[-1]
    xa_width = w_in.shape[1] - 3 * width - heads
    xa_heads = xa_width // xa_hd
    m = k_mem.shape[1]
    w_f = jnp.repeat(w_in[:, 3 * width:3 * width + heads], SUBLANES, axis=1)
    w_all = jnp.concatenate(
        [w_in[:, :3 * width], w_in[:, 3 * width + heads:], w_f,
         jnp.zeros((d_model, LANES - SUBLANES * heads), w_in.dtype)], axis=1).astype(BF16)
    hid = jnp.arange(MXU_TILE) // hd
    head_mean = jnp.where(hid[:, None] == hid[None, :], 1.0 / hd, 0.0).astype(BF16)
    piece = (jnp.arange(LANES, dtype=jnp.int32) % SUBLANES).reshape(1, LANES)
    qg = jnp.tile(q_gain * (LOG2E / math.sqrt(hd)), heads).reshape(1, width)
    kg = jnp.tile(k_gain, heads).reshape(1, width)
    bfp = jnp.zeros((1, LANES), F32).at[0, :SUBLANES * heads].set(jnp.repeat(b_f, SUBLANES))
    row = lambda t: t.reshape(1, -1)
    act_t = jax.ShapeDtypeStruct((b, width, s), BF16)
    return pl.pallas_call(
        functools.partial(_fox_front_kernel, ts=ts, sub=min(ts, MXU_TILE), width=width, heads=heads,
                          xa_heads=xa_heads, xa_hd=xa_hd),
        grid=(b, s // ts),
        in_specs=[
            pl.BlockSpec((1, ts, d_model), lambda i, j: (i, j, 0)),
            _const_spec((1, d_model)),
            _const_spec(w_all.shape),
            _const_spec(head_mean.shape),
            _const_spec((1, width)),
            _const_spec((1, width)),
            _const_spec((1, LANES)),
            _const_spec((1, LANES)),
            pl.BlockSpec((1, m, xa_width), lambda i, j: (i, 0, 0)),
            pl.BlockSpec((1, m, xa_width), lambda i, j: (i, 0, 0)),
            _const_spec((1, xa_hd)),
        ],
        out_specs=[
            pl.BlockSpec((1, width, ts), lambda i, j: (i, 0, j)),
            pl.BlockSpec((1, ts, 2 * width), lambda i, j: (i, j, 0)),
            pl.BlockSpec((1, width, ts), lambda i, j: (i, 0, j)),
            pl.BlockSpec((1, SUBLANES * heads, ts), lambda i, j: (i, 0, j)),
            pl.BlockSpec((1, ts, xa_width), lambda i, j: (i, j, 0)),
        ],
        out_shape=[act_t, jax.ShapeDtypeStruct((b, s, 2 * width), BF16), act_t,
                   jax.ShapeDtypeStruct((b, SUBLANES * heads, s), F32),
                   jax.ShapeDtypeStruct((b, s, xa_width), BF16)],
        scratch_shapes=[pltpu.VMEM((1, LANES), F32)],
        compiler_params=pltpu.CompilerParams(
            dimension_semantics=("arbitrary", "arbitrary"),
            vmem_limit_bytes=VMEM_LIMIT_BYTES),
        name="fox_front",
    )(x, row(norm_g), w_all, head_mean, qg, kg, bfp, piece, k_mem, v_mem, row(xa_qg))


def _fox_attn_kernel(qt_ref, qtn_ref, ka_ref, vt_ref, nb_ref, nbn_ref, o_ref,
                     s0_ref, s1_ref, sf_ref, m_ref, acc_ref, *, t, hd, heads, parts):
    i = pl.program_id(2)
    rowq = lax.broadcasted_iota(jnp.int32, (2 * hd, t), 0)
    row8 = lax.broadcasted_iota(jnp.int32, (SUBLANES, t), 0)
    keys = lax.broadcasted_iota(jnp.int32, (t, t), 0)
    queries = lax.broadcasted_iota(jnp.int32, (t, t), 1)
    tp = t // parts

    def weights(h, q_ref, b_ref):
        slot = h % 2
        qt = q_ref[0, 2 * hd * (h // 2):2 * hd * (h // 2 + 1), :]
        top = jnp.where((rowq >= hd * slot) & (rowq < hd * (slot + 1)), qt, jnp.zeros_like(qt))
        nb = pltpu.roll(b_ref[0, SUBLANES * h:SUBLANES * (h + 1), :], 4, 0)
        ones0 = jnp.where(row8 < 3, 1.0 if slot == 0 else 0.0, 0.0)
        ones1 = jnp.where(row8 < 3, 1.0 if slot == 1 else 0.0, 0.0)
        g0 = jnp.where((row8 >= 4) & (row8 < 7), nb, ones0)
        bot = jnp.concatenate([g0, ones1, jnp.zeros((2 * hd - 2 * SUBLANES, t), F32)], axis=0)
        return jnp.concatenate([top, bot.astype(BF16)], axis=0)

    w_here = [weights(h, qt_ref, nb_ref) for h in range(heads)]
    w_next = [weights(h, qtn_ref, nbn_ref) for h in range(heads)]

    def logits(j, s_ref, w, h, part):
        rows = pl.ds(pl.multiple_of(j * t, t) + tp * part, tp)
        ka = ka_ref[0, rows, 4 * hd * (h // 2):4 * hd * (h // 2 + 1)]
        s_ref[h, tp * part:tp * (part + 1), :] = _dot(ka, w[h]).astype(BF16)

    def block(jc, s_cur, masked, nxt):
        jn, s_next, w = nxt
        for h in range(heads):
            st = s_cur[h]
            if masked:
                st = jnp.where(keys <= queries, st, jnp.asarray(NEG_INF, BF16))
            m = m_ref[h]
            mn = jnp.maximum(m, jnp.max(st, axis=0, keepdims=True).astype(F32))
            m_ref[h] = mn
            mb = mn.astype(BF16)
            pv = None
            for part in range(parts):
                logits(jn, s_next, w, h, part)
                pt = jnp.exp2(st[tp * part:tp * (part + 1), :] - mb)
                cols = pl.ds(pl.multiple_of(jc * t, t) + tp * part, tp)
                vt = jnp.concatenate([vt_ref[0, hd * h:hd * (h + 1), cols], ones_row], axis=0)
                d = _dot(vt, pt)
                pv = d if pv is None else pv + d
            acc_ref[h] = jnp.exp2(m - mn) * acc_ref[h] + pv

    m_ref[...] = jnp.full(m_ref.shape, NEG_INF, F32)
    acc_ref[...] = jnp.zeros_like(acc_ref)
    ones_row = (lax.broadcasted_iota(jnp.int32, (BF16_ROWS, tp), 0) == 0).astype(BF16)

    first_of_next = (0, sf_ref, w_next)

    @pl.when(i == 0)
    def _():
        for h in range(heads):
            for part in range(parts):
                logits(0, sf_ref, w_here, h, part)
        block(0, sf_ref, True, first_of_next)

    @pl.when(i > 0)
    def _():
        block(0, sf_ref, False, (1, s1_ref, w_here))

    def two_blocks(jj, carry):
        j = 2 * jj + 1
        block(j, s1_ref, False, (j + 1, s0_ref, w_here))
        block(j + 1, s0_ref, False, (j + 2, s1_ref, w_here))
        return carry

    lax.fori_loop(0, lax.shift_right_logical(jnp.maximum(i - 1, 0), 1), two_blocks, 0)

    @pl.when(lax.bitwise_and(i, 1) == 1)
    def _():
        block(i, s1_ref, True, first_of_next)

    @pl.when((lax.bitwise_and(i, 1) == 0) & (i > 0))
    def _():
        block(i - 1, s1_ref, False, (i, s0_ref, w_here))
        block(i, s0_ref, True, first_of_next)

    for h in range(heads):
        o_ref[0, hd * h:hd * (h + 1), :] = (acc_ref[h, :hd, :] / acc_ref[h, hd:hd + 1, :]).astype(BF16)


def _fox_attn(qt, kaug, vt, nb, hd, t, heads_per_step):
    b, width, s = qt.shape
    hps = heads_per_step
    nq = s // t
    here = lambda bi, p, i: (bi, p, i)
    nxt = lambda bi, p, i: (bi, p, jnp.minimum(i + 1, nq - 1))
    return pl.pallas_call(
        functools.partial(_fox_attn_kernel, t=t, hd=hd, heads=hps, parts=max(1, t // MXU_TILE)),
        grid=(b, width // (hps * hd), nq),
        in_specs=[
            pl.BlockSpec((1, hps * hd, t), here),
            pl.BlockSpec((1, hps * hd, t), nxt),
            pl.BlockSpec((1, s, 2 * hps * hd), lambda bi, p, i: (bi, 0, p)),
            pl.BlockSpec((1, hps * hd, s), lambda bi, p, i: (bi, p, 0)),
            pl.BlockSpec((1, hps * SUBLANES, t), here),
            pl.BlockSpec((1, hps * SUBLANES, t), nxt),
        ],
        out_specs=pl.BlockSpec((1, hps * hd, t), here),
        out_shape=jax.ShapeDtypeStruct((b, width, s), BF16),
        scratch_shapes=[
            pltpu.VMEM((hps, t, t), BF16),
            pltpu.VMEM((hps, t, t), BF16),
            pltpu.VMEM((hps, t, t), BF16),
            pltpu.VMEM((hps, 1, t), F32),
            pltpu.VMEM((hps, hd + BF16_ROWS, t), F32),
        ],
        compiler_params=pltpu.CompilerParams(
            dimension_semantics=("arbitrary", "arbitrary", "arbitrary"),
            vmem_limit_bytes=VMEM_LIMIT_BYTES),
        name="fox_attn",
    )(qt, qt, kaug, vt, nb, nb)


def _out_mlp_kernel(x_ref, ya_ref, yb_ref, wo_ref, g_ref, w1_ref, w2_ref, o_ref,
                    *, wa, ff_chunk, ya_transposed):
    if ya_transposed:
        mix = lax.dot_general(ya_ref[0], wo_ref[0, :wa, :], (((0,), (0,)), ((), ())),
                              preferred_element_type=F32)
    else:
        mix = _dot(ya_ref[0], wo_ref[0, :wa, :])
    x1 = x_ref[0] + mix + _dot(yb_ref[0], wo_ref[0, wa:, :])
    h = (_rms(x1) * g_ref[0]).astype(BF16)
    d_ff = w1_ref.shape[2]
    o_ref[0] = x1
    for c0 in range(0, d_ff, ff_chunk):
        a = jnp.maximum(_dot(h, w1_ref[0, :, c0:c0 + ff_chunk]), 0.0)
        o_ref[0] += _dot((a * a).astype(BF16), w2_ref[0, c0:c0 + ff_chunk, :])


def _out_mlp(x, ya, yb, w_out, norm_g, w1, w2, layer, tm, ya_transposed):
    b, s, d_model = x.shape
    wb = yb.shape[-1]
    wa = w_out.shape[1] - wb
    d_ff = w1.shape[2]
    of_layer = lambda shape: pl.BlockSpec((1,) + shape[1:], lambda i, j: (layer,) + (0,) * (len(shape) - 1))
    if ya_transposed:
        ya_spec = pl.BlockSpec((1, wa, tm), lambda i, j: (i, 0, j))
    else:
        ya_spec = pl.BlockSpec((1, tm, wa), lambda i, j: (i, j, 0))
    return pl.pallas_call(
        functools.partial(_out_mlp_kernel, wa=wa, ff_chunk=min(d_ff, 1024), ya_transposed=ya_transposed),
        grid=(b, s // tm),
        in_specs=[
            pl.BlockSpec((1, tm, d_model), lambda i, j: (i, j, 0)),
            ya_spec,
            pl.BlockSpec((1, tm, wb), lambda i, j: (i, j, 0)),
            of_layer(w_out.shape),
            of_layer(norm_g.shape),
            of_layer(w1.shape),
            of_layer(w2.shape),
        ],
        out_specs=pl.BlockSpec((1, tm, d_model), lambda i, j: (i, j, 0)),
        out_shape=jax.ShapeDtypeStruct((b, s, d_model), F32),
        compiler_params=pltpu.CompilerParams(
            dimension_semantics=("arbitrary", "arbitrary"),
            vmem_limit_bytes=VMEM_LIMIT_BYTES),
        name="out_mlp",
    )(x, ya, yb, w_out, norm_g, w1, w2)


def kernel(x, mem, norm_mix, norm_mem, w_mem_kv, xa_q_gain, xa_k_gain, w_out, norm_mlp, w_mlp_in, w_mlp_out, w_in_a, conv_w, conv_b, w_rgate, b_rgate, w_igate, b_igate, lru_lambda, w_in_b, b_forget, fox_q_gain, fox_k_gain):
    depth = norm_mix.shape[0]
    s = x.shape[1]
    ts_front = min(s, ROW_TILE)
    t_attn = min(s, ATTN_TILE)
    k_mem, v_mem = _mem_kv(mem, norm_mem, w_mem_kv, xa_k_gain)
    w_out_b, w1_b, w2_b = w_out.astype(BF16), w_mlp_in.astype(BF16), w_mlp_out.astype(BF16)
    g_mlp = norm_mlp.reshape(depth, 1, -1)
    for layer in range(depth):
        j = layer // 2
        if layer % 2 == 0:
            ya, yb = _lru_front(x, norm_mix[layer], w_in_a[j], conv_w[j], conv_b[j], w_rgate[j], b_rgate[j],
                                w_igate[j], b_igate[j], lru_lambda[j], k_mem[layer], v_mem[layer],
                                xa_q_gain[layer], ts_front)
        else:
            qt, kaug, vt, nb, yb = _fox_front(x, norm_mix[layer], w_in_b[j], b_forget[j], fox_q_gain[j],
                                              fox_k_gain[j], k_mem[layer], v_mem[layer], xa_q_gain[layer],
                                              ts_front)
            ya = _fox_attn(qt, kaug, vt, nb, fox_q_gain.shape[-1], t_attn, ATTN_HEADS_PER_STEP)
        x = _out_mlp(x, ya, yb, w_out_b, g_mlp, w1_b, w2_b, layer, ts_front, ya_transposed=layer % 2 == 1)
    return x
```
